```python
import math
import jax, jax.numpy as jnp
from jax import lax
import numpy as np

D_MODEL = 1024
BATCH = 16
SEQ = 4096
DEPTH = 4
DEC_BATCH = 16
DEC_SEQ = 64
PAST_LEN = 2048

CHUNK = 64
Q_BLOCK = 128
SEL_Q_BLOCK = 32
BRANCH_W = D_MODEL // 2
N_BRANCH = 4
A_HEADS = 4
A_DIM = BRANCH_W // (2 * A_HEADS)
B_CH = BRANCH_W
B_WIDTH = 31
C_HEADS = 8
C_DIM = BRANCH_W // C_HEADS
IDX_HEADS = 4
IDX_DIM = 64
TOPK_MAX = 256
D_HEADS = 4
D_VDIM = BRANCH_W // D_HEADS
D_KDIM = D_VDIM // 2
D_CONV = 4
NUM_BUCKETS = 32
MAX_DISTANCE = 128
D_FF = 4 * D_MODEL
EPS = 1e-6

A_QK = A_HEADS * 2 * A_DIM
C_W = C_HEADS * C_DIM
D_QK = D_HEADS * D_KDIM
D_V = D_HEADS * D_VDIM
D_CONV_CH = 2 * D_QK + D_V
IN_SIZES = (A_QK, A_QK, A_QK, 2 * B_CH, C_W, C_W, C_W, IDX_HEADS * IDX_DIM, IDX_DIM, IDX_HEADS, D_CONV_CH, D_V, D_HEADS, D_HEADS)
N_IN = sum(IN_SIZES)

kernel_name = 'hybrid_streaming_encoder_step'


def rms_norm(x, g):
    xf = x.astype(jnp.float32)
    y = xf * lax.rsqrt(jnp.mean(xf * xf, axis=-1, keepdims=True) + EPS)
    return (y * g.astype(jnp.float32)).astype(x.dtype)


def layer_norm(x, g, b):
    xf = x.astype(jnp.float32)
    xc = xf - jnp.mean(xf, axis=-1, keepdims=True)
    y = xc * lax.rsqrt(jnp.mean(xc * xc, axis=-1, keepdims=True) + EPS)
    return (y * g.astype(jnp.float32) + b.astype(jnp.float32)).astype(x.dtype)


def l2norm(x):
    xf = x.astype(jnp.float32)
    return (xf * lax.rsqrt(jnp.sum(xf * xf, axis=-1, keepdims=True) + EPS)).astype(x.dtype)


def rel_bucket(rel):
    nb = NUM_BUCKETS // 2
    max_exact = nb // 2
    n = jnp.abs(rel)
    large = max_exact + (jnp.log(jnp.maximum(n, max_exact).astype(jnp.float32) / max_exact)
                         / math.log(MAX_DISTANCE / max_exact) * (nb - max_exact)).astype(jnp.int32)
    large = jnp.minimum(large, nb - 1)
    return jnp.where(rel > 0, nb, 0) + jnp.where(n < max_exact, n, large)


def causal_dwconv(u, buf, w):
    full = jnp.concatenate([buf, u], axis=1)
    y = lax.conv_general_dilated(full, w[:, None, :], window_strides=(1,), padding='VALID',
                                 dimension_numbers=('NWC', 'WIO', 'NWC'), feature_group_count=u.shape[-1])
    return y, full[:, -(w.shape[0] - 1):]


def sweep(fn, q_arrays, q_pos, block):
    T = q_pos.shape[0]
    nb = T // block
    def split(a):
        return jnp.moveaxis(a.reshape(a.shape[0], nb, block, *a.shape[2:]), 1, 0)
    xs = tuple(split(a) for a in q_arrays) + (q_pos.reshape(nb, block),)
    out = jnp.moveaxis(lax.map(lambda args: fn(*args), xs), 0, 1)
    return out.reshape(out.shape[0], T, *out.shape[3:])


def diff_attn(q, k, v, q_pos, k_pos, lam, bias_tab):
    s = jnp.einsum('bqhmd,bkhmd->bmhqk', q, k).astype(jnp.float32) * (A_DIM ** -0.5)
    bias = bias_tab[rel_bucket(k_pos[None, :] - q_pos[:, None])]
    s = s + jnp.transpose(bias, (2, 0, 1))[None, None].astype(jnp.float32)
    vis = (k_pos[None, :] // CHUNK) <= (q_pos[:, None] // CHUNK)
    p = jax.nn.softmax(jnp.where(vis, s, -jnp.inf), axis=-1)
    attn = p[:, 0] - lam * p[:, 1]
    return jnp.einsum('bhqk,bkhe->bqhe', attn.astype(v.dtype), v)


def dsa_attn(q, qi, wi, q_pos, kv, ki, k_pos, k_sel, bias_tab):
    f32 = jnp.float32
    isc = jax.nn.relu(jnp.einsum('bqhd,bkd->bqhk', qi.astype(f32), ki.astype(f32)))
    isc = jnp.einsum('bqh,bqhk->bqk', wi.astype(f32), isc)
    adm = (k_pos[None, :] // CHUNK) <= (q_pos[:, None] // CHUNK)
    _, idx = lax.top_k(jnp.where(adm[None], isc, -jnp.inf), k_sel)
    sel = jax.vmap(lambda a, i: a[i])(kv, idx)
    ks, vs = sel[..., :C_DIM], sel[..., C_DIM:]
    kp = k_pos[idx]
    rel = kp - q_pos[None, :, None]
    s = jnp.einsum('bqhd,bqjhd->bhqj', q, ks).astype(f32) * (C_DIM ** -0.5)
    s = s + jnp.moveaxis(bias_tab[rel_bucket(rel)], -1, 1).astype(f32)
    valid = (kp // CHUNK) <= (q_pos[None, :, None] // CHUNK)
    p = jax.nn.softmax(jnp.where(valid[:, None], s, -jnp.inf), axis=-1)
    return jnp.einsum('bhqj,bqjhd->bqhd', p.astype(vs.dtype), vs)


def gdn_chunk(S, q, k, v, beta, g):
    C = q.shape[2]
    G = jnp.cumsum(g, axis=-1)
    incl = jnp.tril(jnp.ones((C, C), bool))
    strict = jnp.tril(jnp.ones((C, C), bool), -1)
    diff = G[..., :, None] - G[..., None, :]
    decay = jnp.where(incl, jnp.exp(jnp.where(incl, diff, 0.0)), 0.0)
    kk = jnp.einsum('bhtd,bhsd->bhts', k, k)
    A = jnp.where(strict, beta[..., None] * decay * kk, 0.0)
    eG = jnp.exp(G)[..., None]
    rhs = beta[..., None] * (v - eG * jnp.einsum('bhtd,bhde->bhte', k, S))
    U = lax.linalg.triangular_solve(A, rhs, left_side=True, lower=True, unit_diagonal=True)
    qk = jnp.einsum('bhtd,bhsd->bhts', q, k) * decay
    O = eG * jnp.einsum('bhtd,bhde->bhte', q, S) + jnp.einsum('bhts,bhse->bhte', qk, U)
    GC = G[..., -1:]
    S_new = jnp.exp(GC)[..., None] * S + jnp.einsum('bhtd,bhte->bhde', k * jnp.exp(GC - G)[..., None], U)
    return S_new, O


def gated_delta(q, k, v, beta, g, S0):
    B, T, H, _ = q.shape
    Cc = T if T <= CHUNK else CHUNK
    nc = T // Cc
    def prep(a):
        a = jnp.moveaxis(a.astype(jnp.float32), 2, 1)
        return jnp.moveaxis(a.reshape(B, H, nc, Cc, *a.shape[3:]), 2, 0)
    S, O = lax.scan(lambda s, xs: gdn_chunk(s, *xs), S0, (prep(q), prep(k), prep(v), prep(beta), prep(g)))
    O = jnp.moveaxis(jnp.moveaxis(O, 0, 2).reshape(B, H, T, -1), 1, 2)
    return O.astype(q.dtype), S


def token_mix(h, pos, past, lp, rel_bias):
    B, T, _ = h.shape
    f32 = jnp.float32
    split_at = [int(i) for i in np.cumsum(IN_SIZES)[:-1]]
    (aq, ak, av, bglu, cq, ck, cv, iq, ik, iw, dqkv, dz, db, da) = jnp.split(h @ lp['w_in'], split_at, axis=-1)

    lam_init = lp['lam_init']
    lv = lp['a_lambda'].astype(f32)
    lam = jnp.exp(jnp.sum(lv[0] * lv[1])) - jnp.exp(jnp.sum(lv[2] * lv[3])) + lam_init
    qA = aq.reshape(B, T, A_HEADS, 2, A_DIM)
    kA = ak.reshape(B, T, A_HEADS, 2 * A_DIM)
    vA = av.reshape(B, T, A_HEADS, 2 * A_DIM)
    kA_all = kA if past is None else jnp.concatenate([past['a_k'], kA], axis=1)
    vA_all = vA if past is None else jnp.concatenate([past['a_v'], vA], axis=1)
    L = kA_all.shape[1]
    k_pos = jnp.arange(L, dtype=jnp.int32)
    kA4 = kA_all.reshape(B, L, A_HEADS, 2, A_DIM)
    bias_a = rel_bias[:, :A_HEADS]
    oA = sweep(lambda qb, pb: diff_attn(qb, kA4, vA_all, pb, k_pos, lam, bias_a), (qA,), pos, math.gcd(T, Q_BLOCK))
    oA = (rms_norm(oA, lp['a_norm_g']) * (1.0 - lam_init)).reshape(B, T, BRANCH_W)

    u = bglu[..., :B_CH] * jax.nn.sigmoid(bglu[..., B_CH:])
    b_buf = jnp.zeros((B, B_WIDTH - 1, B_CH), u.dtype) if past is None else past['b_conv']
    ub, b_conv_new = causal_dwconv(u, b_buf, lp['b_dw_w'])
    oB = jax.nn.silu(layer_norm(ub + lp['b_dw_b'], lp['b_ln_g'], lp['b_ln_b']))

    qC = cq.reshape(B, T, C_HEADS, C_DIM)
    kC = ck.reshape(B, T, C_HEADS, C_DIM)
    vC = cv.reshape(B, T, C_HEADS, C_DIM)
    qI = iq.reshape(B, T, IDX_HEADS, IDX_DIM)
    wI = iw * (IDX_HEADS ** -0.5 * IDX_DIM ** -0.5)
    kC_all = kC if past is None else jnp.concatenate([past['c_k'], kC], axis=1)
    vC_all = vC if past is None else jnp.concatenate([past['c_v'], vC], axis=1)
    kI_all = ik if past is None else jnp.concatenate([past['c_kidx'], ik], axis=1)
    k_sel = min(TOPK_MAX, L // 4)
    kvC = jnp.concatenate([kC_all, vC_all], axis=-1)
    bias_c = rel_bias[:, A_HEADS:]
    oC = sweep(lambda qb, qib, wib, pb: dsa_attn(qb, qib, wib, pb, kvC, kI_all, k_pos, k_sel, bias_c),
               (qC, qI, wI), pos, math.gcd(T, SEL_Q_BLOCK)).reshape(B, T, BRANCH_W)

    d_buf = jnp.zeros((B, D_CONV - 1, D_CONV_CH), dqkv.dtype) if past is None else past['d_conv']
    qkv, d_conv_new = causal_dwconv(dqkv, d_buf, lp['d_conv_w'])
    qkv = jax.nn.silu(qkv)
    qD = l2norm(qkv[..., :D_QK].reshape(B, T, D_HEADS, D_KDIM)) * (D_KDIM ** -0.5)
    kD = l2norm(qkv[..., D_QK:2 * D_QK].reshape(B, T, D_HEADS, D_KDIM))
    vD = qkv[..., 2 * D_QK:].reshape(B, T, D_HEADS, D_VDIM)
    beta = jax.nn.sigmoid(db.astype(f32))
    gdec = -jnp.exp(lp['d_a_log'].astype(f32)) * jax.nn.softplus(da.astype(f32) + lp['d_dt_bias'].astype(f32))
    S0 = jnp.zeros((B, D_HEADS, D_KDIM, D_VDIM), f32) if past is None else past['d_state'].astype(f32)
    oD, S_new = gated_delta(qD, kD, vD, beta, gdec, S0)
    oD = (rms_norm(oD, lp['d_norm_g']) * jax.nn.silu(dz.reshape(B, T, D_HEADS, D_VDIM))).reshape(B, T, BRANCH_W)

    merged = None
    for m, o in enumerate((oA, oB, oC, oD)):
        term = jax.nn.sigmoid(h @ lp['w_gate'][m]) * (o @ lp['w_br'][m])
        merged = term if merged is None else merged + term
    out = merged @ lp['w_out']
    new = {'a_k': kA, 'a_v': vA, 'c_k': kC, 'c_v': vC, 'c_kidx': ik,
           'b_conv': b_conv_new, 'd_conv': d_conv_new, 'd_state': S_new.astype(h.dtype)}
    return out, new


def run_layer(x, c, pos, past, lp, rel_bias):
    mod = jax.nn.silu(c) @ lp['ada_w'] + lp['ada_b']
    sh1, sc1, gt1, sh2, sc2, gt2 = [m[:, None, :] for m in jnp.split(mod, 6, axis=-1)]
    g = lp['norm_g']
    h = rms_norm(x, g[0]) * (1.0 + sc1) + sh1
    mix, new = token_mix(h, pos, past, lp, rel_bias)
    x = x + gt1 * rms_norm(mix, g[1])
    h = rms_norm(x, g[2]) * (1.0 + sc2) + sh2
    f = jnp.square(jax.nn.relu(h @ lp['mlp_w1'])) @ lp['mlp_w2']
    x = x + gt2 * rms_norm(f, g[3])
    return x, new


def setup_inputs(seed: int = 0) -> dict:
    key = jax.random.key(seed)
    ks = jax.random.split(key, 40)
    f32 = jnp.float32
    D = D_MODEL
    def nrm(i, shape, s=1.0):
        return jax.random.normal(ks[i], shape, f32) * s
    def gain(i, shape):
        return 1.0 + 0.05 * jax.random.normal(ks[i], shape, f32)
    dt = jnp.exp(jax.random.uniform(ks[24], (DEPTH, D_HEADS), f32, math.log(1e-3), math.log(1e-1)))
    return {
        'x_prompt': nrm(0, (BATCH, SEQ, D)),
        'x_sample': nrm(1, (DEC_BATCH, DEC_SEQ, D)),
        'c_prompt': nrm(2, (BATCH, D)),
        'c_sample': nrm(3, (DEC_BATCH, D)),
        'cache_a_k': nrm(4, (DEPTH, DEC_BATCH, PAST_LEN, A_HEADS, 2 * A_DIM)),
        'cache_a_v': nrm(5, (DEPTH, DEC_BATCH, PAST_LEN, A_HEADS, 2 * A_DIM)),
        'cache_c_k': nrm(6, (DEPTH, DEC_BATCH, PAST_LEN, C_HEADS, C_DIM)),
        'cache_c_v': nrm(7, (DEPTH, DEC_BATCH, PAST_LEN, C_HEADS, C_DIM)),
        'cache_c_kidx': nrm(8, (DEPTH, DEC_BATCH, PAST_LEN, IDX_DIM)),
        'state_b_conv': nrm(9, (DEPTH, DEC_BATCH, B_WIDTH - 1, B_CH)),
        'state_d_conv': nrm(10, (DEPTH, DEC_BATCH, D_CONV - 1, D_CONV_CH)),
        'state_d_state': nrm(11, (DEPTH, DEC_BATCH, D_HEADS, D_KDIM, D_VDIM), 0.3),
        'rel_bias': nrm(12, (NUM_BUCKETS, A_HEADS + C_HEADS), 0.5),
        'ada_w': nrm(13, (DEPTH, D, 6 * D), 0.5 * D ** -0.5),
        'ada_b': nrm(14, (DEPTH, 6 * D), 0.05),
        'norm_g': gain(15, (DEPTH, 4, D)),
        'w_in': nrm(16, (DEPTH, D, N_IN), D ** -0.5),
        'a_lambda': nrm(17, (DEPTH, 4, A_DIM), 0.1),
        'a_norm_g': gain(18, (DEPTH, 2 * A_DIM)),
        'b_dw_w': nrm(19, (DEPTH, B_WIDTH, B_CH), B_WIDTH ** -0.5),
        'b_dw_b': nrm(20, (DEPTH, B_CH), 0.02),
        'b_ln_g': gain(21, (DEPTH, B_CH)),
        'b_ln_b': nrm(22, (DEPTH, B_CH), 0.02),
        'd_conv_w': nrm(23, (DEPTH, D_CONV, D_CONV_CH), D_CONV ** -0.5),
        'd_a_log': jnp.log(jax.random.uniform(ks[25], (DEPTH, D_HEADS), f32, 1.0, 16.0)),
        'd_dt_bias': dt + jnp.log(-jnp.expm1(-dt)),
        'd_norm_g': gain(26, (DEPTH, D_VDIM)),
        'w_gate': nrm(27, (DEPTH, N_BRANCH, D, D), D ** -0.5),
        'w_br': nrm(28, (DEPTH, N_BRANCH, BRANCH_W, D), BRANCH_W ** -0.5),
        'w_out': nrm(29, (DEPTH, D, D), D ** -0.5),
        'mlp_w1': nrm(30, (DEPTH, D, D_FF), D ** -0.5),
        'mlp_w2': nrm(31, (DEPTH, D_FF, D), D_FF ** -0.5),
    }


def reference(x_prompt, x_sample, c_prompt, c_sample, cache_a_k, cache_a_v, cache_c_k, cache_c_v,
              cache_c_kidx, state_b_conv, state_d_conv, state_d_state, rel_bias, ada_w, ada_b, norm_g,
              w_in, a_lambda, a_norm_g, b_dw_w, b_dw_b, b_ln_g, b_ln_b, d_conv_w, d_a_log, d_dt_bias,
              d_norm_g, w_gate, w_br, w_out, mlp_w1, mlp_w2):
    P = cache_a_k.shape[2]
    pos_p = jnp.arange(x_prompt.shape[1], dtype=jnp.int32)
    pos_s = P + jnp.arange(x_sample.shape[1], dtype=jnp.int32)
    names = ('a_k', 'a_v', 'c_k', 'c_v', 'c_kidx', 'b_conv', 'd_conv', 'd_state')
    new_p = {n: [] for n in names}
    new_s = {n: [] for n in names}
    xp, xs = x_prompt, x_sample
    for l in range(DEPTH):
        lp = {'ada_w': ada_w[l], 'ada_b': ada_b[l], 'norm_g': norm_g[l], 'w_in': w_in[l],
              'a_lambda': a_lambda[l], 'a_norm_g': a_norm_g[l], 'lam_init': 0.8 - 0.6 * math.exp(-0.3 * l),
              'b_dw_w': b_dw_w[l], 'b_dw_b': b_dw_b[l], 'b_ln_g': b_ln_g[l], 'b_ln_b': b_ln_b[l],
              'd_conv_w': d_conv_w[l], 'd_a_log': d_a_log[l], 'd_dt_bias': d_dt_bias[l], 'd_norm_g': d_norm_g[l],
              'w_gate': w_gate[l], 'w_br': w_br[l], 'w_out': w_out[l], 'mlp_w1': mlp_w1[l], 'mlp_w2': mlp_w2[l]}
        past = {'a_k': cache_a_k[l], 'a_v': cache_a_v[l], 'c_k': cache_c_k[l], 'c_v': cache_c_v[l],
                'c_kidx': cache_c_kidx[l], 'b_conv': state_b_conv[l], 'd_conv': state_d_conv[l],
                'd_state': state_d_state[l]}
        xp, sp = run_layer(xp, c_prompt, pos_p, None, lp, rel_bias)
        xs, ss = run_layer(xs, c_sample, pos_s, past, lp, rel_bias)
        for n in names:
            new_p[n].append(sp[n])
            new_s[n].append(ss[n])
    return (xp, xs,
            jnp.stack(new_p['a_k']), jnp.stack(new_p['a_v']), jnp.stack(new_p['c_k']), jnp.stack(new_p['c_v']),
            jnp.stack(new_p['c_kidx']), jnp.stack(new_p['b_conv']), jnp.stack(new_p['d_conv']), jnp.stack(new_p['d_state']),
            jnp.stack(new_s['a_k']), jnp.stack(new_s['a_v']), jnp.stack(new_s['c_k']), jnp.stack(new_s['c_v']),
            jnp.stack(new_s['c_kidx']), jnp.stack(new_s['b_conv']), jnp.stack(new_s['d_conv']), jnp.stack(new_s['d_state']))
```

```python
import functools
import math

import numpy as np
import jax
import jax.numpy as jnp
from jax import lax
from jax.experimental import pallas as pl
from jax.experimental.pallas import tpu as pltpu

F32 = jnp.float32
BF16 = jnp.bfloat16
HIGHEST = lax.Precision.HIGHEST

D_MODEL = 1024
CHUNK = 64
BRANCH_W = D_MODEL // 2
N_BRANCH = 4
A_HEADS = 4
A_DIM = 64
B_CH = BRANCH_W
B_WIDTH = 31
C_HEADS = 8
C_DIM = 64
IDX_HEADS = 4
IDX_DIM = 64
TOPK_MAX = 256
D_HEADS = 4
D_VDIM = 128
D_KDIM = 64
D_CONV = 4
NUM_BUCKETS = 32
MAX_DISTANCE = 128
D_FF = 4 * D_MODEL
EPS = 1e-6
A_QK = A_HEADS * 2 * A_DIM
C_W = C_HEADS * C_DIM
D_QK = D_HEADS * D_KDIM
D_V = D_HEADS * D_VDIM
D_CONV_CH = 2 * D_QK + D_V
IN_SIZES = (A_QK, A_QK, A_QK, 2 * B_CH, C_W, C_W, C_W, IDX_HEADS * IDX_DIM, IDX_DIM, IDX_HEADS,
            D_CONV_CH, D_V, D_HEADS, D_HEADS)

LANES = 128
NEG = -1e30
INT_MIN = -2 ** 31
VMEM_LIMIT = 56 * 1024 * 1024

COL_AQ, COL_AK, COL_AV = 0, 512, 1024
COL_BGLU = 1536
COL_CQ, COL_CK, COL_CV = 2560, 3072, 3584
COL_DQKV = 4096
COL_DZ = 5120
COL_IQ = 5632
COL_MISC = 5888
COL_GATE = 6144
N_PROJ = 10240
MISC_IW, MISC_DB, MISC_DA = 64, 68, 72

FLASH_T = 256
SEL_TQ = 128


def _cparams(sem):
    return pltpu.CompilerParams(dimension_semantics=sem, vmem_limit_bytes=VMEM_LIMIT)


def _dot(a, b, precision=None):
    return jnp.dot(a, b, preferred_element_type=F32, precision=precision)


def _dot_nt(a, b):
    return lax.dot_general(a, b, (((1,), (1,)), ((), ())), preferred_element_type=F32)


def _dot_tn(a, b):
    return lax.dot_general(a, b, (((0,), (0,)), ((), ())), preferred_element_type=F32)


def _rms(x, g):
    return x * lax.rsqrt(jnp.mean(x * x, axis=-1, keepdims=True) + EPS) * g


def _silu(x):
    return x * jax.nn.sigmoid(x)


def _chunk_of(pos):
    return jnp.right_shift(pos, int(math.log2(CHUNK)))


def _mod_kernel(c_ref, w_ref, b_ref, o_ref):
    s = _silu(c_ref[...])
    o_ref[0] = _dot(s.astype(BF16), w_ref[0].astype(BF16)) + b_ref[0]


def adaln_mod(c_all, ada_w, ada_b):
    depth, d, n = ada_w.shape
    bc = c_all.shape[0]
    tn = 1024
    return pl.pallas_call(
        _mod_kernel,
        grid=(depth, n // tn),
        in_specs=[pl.BlockSpec((bc, d), lambda l, j: (0, 0)),
                  pl.BlockSpec((1, d, tn), lambda l, j: (l, 0, j)),
                  pl.BlockSpec((1, 1, tn), lambda l, j: (l, 0, j))],
        out_specs=pl.BlockSpec((1, bc, tn), lambda l, j: (l, 0, j)),
        out_shape=jax.ShapeDtypeStruct((depth, bc, n), F32),
        compiler_params=_cparams(("arbitrary", "arbitrary")),
        name="adaln_mod",
    )(c_all, ada_w, ada_b.reshape(depth, 1, n))


def _bias_kernel(tab_ref, o_ref, *, q0, k0):
    h = pl.program_id(0)
    tq, w = o_ref.shape[1], o_ref.shape[2]
    row = lax.broadcasted_iota(jnp.int32, (tq, w), 0)
    col = lax.broadcasted_iota(jnp.int32, (tq, w), 1)
    rel = (col + k0) - (row + q0)
    nb = NUM_BUCKETS // 2
    max_exact = nb // 2
    n = jnp.abs(rel)
    large = max_exact + (jnp.log(jnp.maximum(n, max_exact).astype(F32) / max_exact)
                         / math.log(MAX_DISTANCE / max_exact) * (nb - max_exact)).astype(jnp.int32)
    large = jnp.minimum(large, nb - 1)
    bucket = jnp.where(rel > 0, nb, 0) + jnp.where(n < max_exact, n, large)
    val = jnp.zeros((tq, w), F32)
    for bk in range(NUM_BUCKETS):
        val = jnp.where(bucket == bk, tab_ref[bk, h], val)
    o_ref[0] = val


def bias_tiles(rel_bias, tq, w, q0, k0):
    nh = rel_bias.shape[1]
    return pl.pallas_call(
        functools.partial(_bias_kernel, q0=q0, k0=k0),
        grid=(nh,),
        in_specs=[pl.BlockSpec(memory_space=pltpu.SMEM)],
        out_specs=pl.BlockSpec((1, tq, w), lambda h: (h, 0, 0)),
        out_shape=jax.ShapeDtypeStruct((nh, tq, w), F32),
        compiler_params=_cparams(("arbitrary",)),
        name="bias_tiles",
    )(rel_bias)


def _inproj_kernel(x_ref, sh_ref, sc_ref, g_ref, w_ref, o_ref, h_ref):
    bt, tt, d = x_ref.shape

    @pl.when(pl.program_id(2) == 0)
    def _():
        h = _rms(x_ref[...], g_ref[...]) * (1.0 + sc_ref[...]) + sh_ref[...]
        h_ref[...] = h.reshape(bt * tt, d).astype(BF16)

    o_ref[...] = _dot(h_ref[...], w_ref[...]).reshape(o_ref.shape)


def inproj(x, mod, g, w_all, bt, tt):
    b, t, d = x.shape
    n = w_all.shape[1]
    tn = 1024
    return pl.pallas_call(
        _inproj_kernel,
        grid=(b // bt, t // tt, n // tn),
        in_specs=[pl.BlockSpec((bt, tt, d), lambda i, j, k: (i, j, 0)),
                  pl.BlockSpec((bt, 1, d), lambda i, j, k: (i, 0, 0)),
                  pl.BlockSpec((bt, 1, d), lambda i, j, k: (i, 0, 1)),
                  pl.BlockSpec((1, d), lambda i, j, k: (0, 0)),
                  pl.BlockSpec((d, tn), lambda i, j, k: (0, k))],
        out_specs=pl.BlockSpec((bt, tt, tn), lambda i, j, k: (i, j, k)),
        out_shape=jax.ShapeDtypeStruct((b, t, n), F32),
        scratch_shapes=[pltpu.VMEM((bt * tt, d), BF16)],
        compiler_params=_cparams(("arbitrary", "arbitrary", "arbitrary")),
        name="inproj",
    )(x, mod, mod, g, w_all)


def _flash_update(m, qm, kb, vb, bias, add_mask, keep, m_ref, l_ref, acc_ref):
    s = _dot_nt(qm, kb) + bias
    if add_mask is not None:
        s = s + add_mask
    if keep is not None:
        s = jnp.where(keep, s, NEG)
    m_prev = m_ref[m]
    m_new = jnp.maximum(m_prev, jnp.max(s, axis=-1, keepdims=True))
    alpha = jnp.exp(m_prev - m_new)
    p = jnp.exp(s - m_new)
    l_ref[m] = alpha * l_ref[m] + jnp.sum(p, axis=-1, keepdims=True)
    acc_ref[m] = alpha * acc_ref[m] + _dot(p.astype(BF16), vb)
    m_ref[m] = m_new


def _flash_finish(diff, lam_ref, alam_ref, g_ref, o_ref, l_ref, acc_ref, lane):
    o0 = acc_ref[0] / l_ref[0]
    o1 = acc_ref[1] / l_ref[1]
    if diff:
        lv = alam_ref[...]
        lam_init = lam_ref[0]
        lam = (jnp.exp(jnp.sum(lv[0:1] * lv[1:2], axis=-1, keepdims=True))
               - jnp.exp(jnp.sum(lv[2:3] * lv[3:4], axis=-1, keepdims=True)) + lam_init)
        o = o0 - lam * o1
        o = _rms(o, g_ref[...]) * (1.0 - lam_init)
    else:
        o = jnp.where(lane < C_DIM, o0, o1)
    o_ref[0] = o


def _flash_init(q_ref, m_ref, l_ref, acc_ref):
    tq = q_ref.shape[1]
    lane = lax.broadcasted_iota(jnp.int32, (1, LANES), 1)
    q = q_ref[0] * (A_DIM ** -0.5)
    q0 = jnp.where(lane < A_DIM, q, 0.0).astype(BF16)
    q1 = jnp.where(lane >= A_DIM, q, 0.0).astype(BF16)
    m_ref[...] = jnp.full(m_ref.shape, NEG, F32)
    l_ref[...] = jnp.zeros(l_ref.shape, F32)
    acc_ref[...] = jnp.zeros(acc_ref.shape, F32)
    return lane, (q0, q1)


def _flash_causal_kernel(*refs, diff, has_mask, col0):
    tab_ref, lam_ref, q_ref, k_ref, v_ref, bias_ref = refs[:6]
    pos = 6
    mask_ref = None
    if has_mask:
        mask_ref = refs[pos]
        pos += 1
    alam_ref = g_ref = None
    if diff:
        alam_ref, g_ref = refs[pos], refs[pos + 1]
        pos += 2
    o_ref, m_ref, l_ref, acc_ref = refs[pos:pos + 4]
    t = q_ref.shape[1]
    nb = bias_ref.shape[0]
    p = pl.program_id(1)
    i = pl.program_id(2)
    lane, qs = _flash_init(q_ref, m_ref, l_ref, acc_ref)
    far = [tab_ref[NUM_BUCKETS // 2 - 1, col0 + nb * p + min(m, nb - 1)] for m in range(2)]

    def tile(kj, biases, keep):
        ks = pl.multiple_of(kj * t, t)
        kb = k_ref[0, pl.ds(ks, t), :].astype(BF16)
        vb = v_ref[0, pl.ds(ks, t), :].astype(BF16)
        add = mask_ref[0, :, pl.ds(ks, t)].astype(F32) if has_mask else None
        for m in range(2):
            _flash_update(m, qs[m], kb, vb, biases[m], add, keep, m_ref, l_ref, acc_ref)

    def far_body(kj, carry):
        tile(kj, far, None)
        return carry

    lax.fori_loop(0, jnp.maximum(i - 1, 0), far_body, 0)

    @pl.when(i >= 1)
    def _():
        tile(i - 1, [bias_ref[min(m, nb - 1), :, 0:t] for m in range(2)], None)

    row = lax.broadcasted_iota(jnp.int32, (t, t), 0)
    col = lax.broadcasted_iota(jnp.int32, (t, t), 1)
    keep = _chunk_of(col) <= _chunk_of(row)
    tile(i, [bias_ref[min(m, nb - 1), :, t:2 * t] for m in range(2)], keep)
    _flash_finish(diff, lam_ref, alam_ref, g_ref, o_ref, l_ref, acc_ref, lane)


def flash_causal(proj, near_bias, rel_bias, lam_init, *, diff, qcol, kcol, vcol, bias_col0,
                 mask=None, a_lambda=None, a_norm_g=None):
    b, t_all, _ = proj.shape
    t = FLASH_T
    npair = 4
    nb = 1 if diff else 2
    qb, kb, vb = qcol // LANES, kcol // LANES, vcol // LANES
    bb0 = bias_col0 // nb
    in_specs = [pl.BlockSpec(memory_space=pltpu.SMEM),
                pl.BlockSpec(memory_space=pltpu.SMEM),
                pl.BlockSpec((1, t, LANES), lambda bi, p, i: (bi, i, qb + p)),
                pl.BlockSpec((1, t_all, LANES), lambda bi, p, i: (bi, 0, kb + p)),
                pl.BlockSpec((1, t_all, LANES), lambda bi, p, i: (bi, 0, vb + p)),
                pl.BlockSpec((nb, t, 2 * t), lambda bi, p, i: (bb0 + p, 0, 0))]
    args = [rel_bias, lam_init, proj, proj, proj, near_bias]
    if mask is not None:
        in_specs.append(pl.BlockSpec((1, t, t_all), lambda bi, p, i: (bi, i, 0)))
        args.append(mask)
    if diff:
        in_specs += [pl.BlockSpec((4, A_DIM), lambda bi, p, i: (0, 0)),
                     pl.BlockSpec((1, LANES), lambda bi, p, i: (0, 0))]
        args += [a_lambda, a_norm_g]
    return pl.pallas_call(
        functools.partial(_flash_causal_kernel, diff=diff, has_mask=mask is not None, col0=bias_col0),
        grid=(b, npair, t_all // t),
        in_specs=in_specs,
        out_specs=pl.BlockSpec((1, t, LANES), lambda bi, p, i: (bi, i, p)),
        out_shape=jax.ShapeDtypeStruct((b, t_all, npair * LANES), F32),
        scratch_shapes=[pltpu.VMEM((2, t, 1), F32), pltpu.VMEM((2, t, 1), F32), pltpu.VMEM((2, t, LANES), F32)],
        compiler_params=_cparams(("arbitrary", "arbitrary", "arbitrary")),
        name="flash_diff" if diff else "flash_sel",
    )(*args)


def _flash_full_kernel(*refs, diff, has_mask, tk):
    lam_ref, q_ref, kc_ref, vc_ref, kn_ref, vn_ref, bias_ref = refs[:7]
    pos = 7
    mask_ref = None
    if has_mask:
        mask_ref = refs[pos]
        pos += 1
    alam_ref = g_ref = None
    if diff:
        alam_ref, g_ref = refs[pos], refs[pos + 1]
        pos += 2
    o_ref, m_ref, l_ref, acc_ref = refs[pos:pos + 4]
    past = kc_ref.shape[2]
    tn = kn_ref.shape[1]
    nb = bias_ref.shape[0]
    lane, qs = _flash_init(q_ref, m_ref, l_ref, acc_ref)

    def tile(kb, vb, c0, width):
        add = mask_ref[0, :, c0:c0 + width].astype(F32) if has_mask else None
        for m in range(2):
            _flash_update(m, qs[m], kb, vb, bias_ref[min(m, nb - 1), :, c0:c0 + width], add, None,
                          m_ref, l_ref, acc_ref)

    for j in range(past // tk):
        tile(kc_ref[0, 0, j * tk:(j + 1) * tk, :].astype(BF16), vc_ref[0, 0, j * tk:(j + 1) * tk, :].astype(BF16),
             j * tk, tk)
    tile(kn_ref[0].astype(BF16), vn_ref[0].astype(BF16), past, tn)
    _flash_finish(diff, lam_ref, alam_ref, g_ref, o_ref, l_ref, acc_ref, lane)


def flash_full(proj, k_cache, v_cache, layer, full_bias, lam_init, *, diff, qcol, kcol, vcol, bias_col0,
               mask=None, a_lambda=None, a_norm_g=None):
    b, tq, _ = proj.shape
    past = k_cache.shape[2]
    npair = 4
    nb = 1 if diff else 2
    ltot = past + tq
    qb, kb, vb = qcol // LANES, kcol // LANES, vcol // LANES
    bb0 = bias_col0 // nb
    in_specs = [pl.BlockSpec(memory_space=pltpu.SMEM),
                pl.BlockSpec((1, tq, LANES), lambda bi, p: (bi, 0, qb + p)),
                pl.BlockSpec((1, 1, past, LANES), lambda bi, p: (layer, bi, 0, p)),
                pl.BlockSpec((1, 1, past, LANES), lambda bi, p: (layer, bi, 0, p)),
                pl.BlockSpec((1, tq, LANES), lambda bi, p: (bi, 0, kb + p)),
                pl.BlockSpec((1, tq, LANES), lambda bi, p: (bi, 0, vb + p)),
                pl.BlockSpec((nb, tq, ltot), lambda bi, p: (bb0 + p, 0, 0))]
    args = [lam_init, proj, k_cache, v_cache, proj, proj, full_bias]
    if mask is not None:
        in_specs.append(pl.BlockSpec((1, tq, mask.shape[2]), lambda bi, p: (bi, 0, 0)))
        args.append(mask)
    if diff:
        in_specs += [pl.BlockSpec((4, A_DIM), lambda bi, p: (0, 0)),
                     pl.BlockSpec((1, LANES), lambda bi, p: (0, 0))]
        args += [a_lambda, a_norm_g]
    return pl.pallas_call(
        functools.partial(_flash_full_kernel, diff=diff, has_mask=mask is not None, tk=256),
        grid=(b, npair),
        in_specs=in_specs,
        out_specs=pl.BlockSpec((1, tq, LANES), lambda bi, p: (bi, 0, p)),
        out_shape=jax.ShapeDtypeStruct((b, tq, npair * LANES), F32),
        scratch_shapes=[pltpu.VMEM((2, tq, 1), F32), pltpu.VMEM((2, tq, 1), F32), pltpu.VMEM((2, tq, LANES), F32)],
        compiler_params=_cparams(("arbitrary", "arbitrary")),
        name="flash_diff_full" if diff else "flash_sel_full",
    )(*args)


def _index_scores(qi, w, kb):
    isc = None
    for h in range(IDX_HEADS):
        qh = qi[:, h * IDX_DIM:(h + 1) * IDX_DIM].astype(BF16)
        term = w[:, h:h + 1] * jnp.maximum(_dot_nt(qh, kb), 0.0)
        isc = term if isc is None else isc + term
    return isc + 0.0


def _sortable(x):
    bits = lax.bitcast_convert_type(x, jnp.int32)
    return jnp.where(bits < 0, bits ^ jnp.int32(0x7FFFFFFF), bits)


def _select_topk(keys_ref, nkb, k_sel, vis_fn, out_ref):
    tq = keys_ref.shape[0]

    def count(pred_fn):
        def body(j, acc):
            blk = keys_ref[:, pl.ds(pl.multiple_of(j * LANES, LANES), LANES)]
            return acc + jnp.where(pred_fn(blk), 1, 0)
        acc = lax.fori_loop(0, nkb, body, jnp.zeros((tq, LANES), jnp.int32))
        return jnp.sum(acc, axis=-1, keepdims=True)

    zero = jnp.zeros((tq, 1), jnp.int32)
    t0 = jnp.where(count(lambda blk: blk >= zero) >= k_sel, zero, jnp.full((tq, 1), INT_MIN, jnp.int32))

    def bit_body(it, t):
        cand = t + jnp.left_shift(jnp.int32(1), 30 - it)
        return jnp.where(count(lambda blk: blk >= cand) >= k_sel, cand, t)

    thr = lax.fori_loop(0, 31, bit_body, t0)
    need = (k_sel - count(lambda blk: blk > thr)).astype(F32)
    r = lax.broadcasted_iota(jnp.int32, (LANES, LANES), 0)
    c = lax.broadcasted_iota(jnp.int32, (LANES, LANES), 1)
    triu = jnp.where(r <= c, 1.0, 0.0).astype(BF16)

    def mask_body(j, offs):
        cs = pl.multiple_of(j * LANES, LANES)
        blk = keys_ref[:, pl.ds(cs, LANES)]
        eq = blk == thr
        pre = _dot(jnp.where(eq, 1.0, 0.0).astype(BF16), triu)
        sel = (blk > thr) | (eq & (offs + pre <= need))
        sel = sel & vis_fn(j)
        out_ref[0, :, pl.ds(cs, LANES)] = jnp.where(sel, 0.0, NEG).astype(out_ref.dtype)
        return offs + pre[:, LANES - 1:LANES]

    lax.fori_loop(0, nkb, mask_body, jnp.zeros((tq, 1), F32))


def _select_causal_kernel(iq_ref, mq_ref, mk_ref, o_ref, keys_ref, *, k_sel):
    tq = iq_ref.shape[1]
    i = pl.program_id(1)
    per = FLASH_T // LANES
    nkb = per * ((i * tq) // FLASH_T + 1)
    qi = iq_ref[0]
    w = mq_ref[0][:, MISC_IW:MISC_IW + IDX_HEADS] * (IDX_HEADS ** -0.5 * IDX_DIM ** -0.5)
    row = lax.broadcasted_iota(jnp.int32, (tq, LANES), 0) + i * tq
    col = lax.broadcasted_iota(jnp.int32, (tq, LANES), 1)

    def vis_fn(j):
        return _chunk_of(col + j * LANES) <= _chunk_of(row)

    def score_body(j, carry):
        cs = pl.multiple_of(j * LANES, LANES)
        kb = mk_ref[0, pl.ds(cs, LANES), :][:, 0:IDX_DIM].astype(BF16)
        key = _sortable(_index_scores(qi, w, kb))
        keys_ref[:, pl.ds(cs, LANES)] = jnp.where(vis_fn(j), key, INT_MIN)
        return carry

    lax.fori_loop(0, nkb, score_body, 0)
    _select_topk(keys_ref, nkb, k_sel, vis_fn, o_ref)


def select_causal(proj):
    b, t, _ = proj.shape
    tq = SEL_TQ
    k_sel = min(TOPK_MAX, t // 4)
    return pl.pallas_call(
        functools.partial(_select_causal_kernel, k_sel=k_sel),
        grid=(b, t // tq),
        in_specs=[pl.BlockSpec((1, tq, IDX_HEADS * IDX_DIM), lambda bi, i: (bi, i, COL_IQ // (IDX_HEADS * IDX_DIM))),
                  pl.BlockSpec((1, tq, LANES), lambda bi, i: (bi, i, COL_MISC // LANES)),
                  pl.BlockSpec((1, t, LANES), lambda bi, i: (bi, 0, COL_MISC // LANES))],
        out_specs=pl.BlockSpec((1, tq, t), lambda bi, i: (bi, i, 0)),
        out_shape=jax.ShapeDtypeStruct((b, t, t), BF16),
        scratch_shapes=[pltpu.VMEM((tq, t), jnp.int32)],
        compiler_params=_cparams(("arbitrary", "arbitrary")),
        name="select_causal",
    )(proj, proj, proj)


def _select_full_kernel(iq_ref, mq_ref, kc_ref, o_ref, keys_ref, *, k_sel):
    tq = iq_ref.shape[1]
    past = kc_ref.shape[2]
    ltot = past + tq
    nkb = keys_ref.shape[1] // LANES
    qi = iq_ref[0]
    misc = mq_ref[0]
    w = misc[:, MISC_IW:MISC_IW + IDX_HEADS] * (IDX_HEADS ** -0.5 * IDX_DIM ** -0.5)
    col = lax.broadcasted_iota(jnp.int32, (tq, LANES), 1)

    def vis_fn(j):
        return (col + j * LANES) < ltot

    for j in range(past // LANES):
        kb = kc_ref[0, 0, j * LANES:(j + 1) * LANES, :].astype(BF16)
        keys_ref[:, j * LANES:(j + 1) * LANES] = _sortable(_index_scores(qi, w, kb))
    keys_ref[:, past:] = jnp.full((tq, keys_ref.shape[1] - past), INT_MIN, jnp.int32)
    kb = misc[:, 0:IDX_DIM].astype(BF16)
    keys_ref[:, past:ltot] = _sortable(_index_scores(qi, w, kb))
    _select_topk(keys_ref, nkb, k_sel, vis_fn, o_ref)


def select_full(proj, kidx_cache, layer):
    b, tq, _ = proj.shape
    past = kidx_cache.shape[2]
    ltot = past + tq
    lpad = -(-ltot // LANES) * LANES
    k_sel = min(TOPK_MAX, ltot // 4)
    return pl.pallas_call(
        functools.partial(_select_full_kernel, k_sel=k_sel),
        grid=(b,),
        in_specs=[pl.BlockSpec((1, tq, IDX_HEADS * IDX_DIM), lambda bi: (bi, 0, COL_IQ // (IDX_HEADS * IDX_DIM))),
                  pl.BlockSpec((1, tq, LANES), lambda bi: (bi, 0, COL_MISC // LANES)),
                  pl.BlockSpec((1, 1, past, IDX_DIM), lambda bi: (layer, bi, 0, 0))],
        out_specs=pl.BlockSpec((1, tq, lpad), lambda bi: (bi, 0, 0)),
        out_shape=jax.ShapeDtypeStruct((b, tq, lpad), BF16),
        scratch_shapes=[pltpu.VMEM((tq, lpad), jnp.int32)],
        compiler_params=_cparams(("arbitrary",)),
        name="select_full",
    )(proj, proj, kidx_cache)


CONVB_ROWS = 64
CONVB_HEAD = 32


def _convb_kernel(a_ref, gt_ref, init_ref, w_ref, dwb_ref, lng_ref, lnb_ref, o_ref, tail_ref, f_ref):
    tt = a_ref.shape[1]
    lead = CONVB_HEAD - (B_WIDTH - 1)

    @pl.when(pl.program_id(1) == 0)
    def _():
        f_ref[0:CONVB_HEAD, :] = init_ref[0]

    f_ref[CONVB_HEAD:CONVB_HEAD + tt, :] = a_ref[0] * jax.nn.sigmoid(gt_ref[0])
    for r in range(tt // CONVB_ROWS):
        acc = None
        for k in range(B_WIDTH):
            s0 = r * CONVB_ROWS + lead + k
            term = w_ref[k:k + 1, :] * f_ref[s0:s0 + CONVB_ROWS, :]
            acc = term if acc is None else acc + term
        y = acc + dwb_ref[...]
        yc = y - jnp.mean(y, axis=-1, keepdims=True)
        yn = yc * lax.rsqrt(jnp.mean(yc * yc, axis=-1, keepdims=True) + EPS) * lng_ref[...] + lnb_ref[...]
        o_ref[0, r * CONVB_ROWS:(r + 1) * CONVB_ROWS, :] = _silu(yn)
    last = f_ref[tt:tt + CONVB_HEAD, :]
    tail_ref[0] = last
    f_ref[0:CONVB_HEAD, :] = last


def conv_module(proj, init, w, dwb, lng, lnb, tt):
    b, t, _ = proj.shape
    cb = COL_BGLU // B_CH
    vec = pl.BlockSpec((1, B_CH), lambda bi, j: (0, 0))
    return pl.pallas_call(
        _convb_kernel,
        grid=(b, t // tt),
        in_specs=[pl.BlockSpec((1, tt, B_CH), lambda bi, j: (bi, j, cb)),
                  pl.BlockSpec((1, tt, B_CH), lambda bi, j: (bi, j, cb + 1)),
                  pl.BlockSpec((1, CONVB_HEAD, B_CH), lambda bi, j: (bi, 0, 0)),
                  pl.BlockSpec((B_WIDTH, B_CH), lambda bi, j: (0, 0)),
                  vec, vec, vec],
        out_specs=[pl.BlockSpec((1, tt, B_CH), lambda bi, j: (bi, j, 0)),
                   pl.BlockSpec((1, CONVB_HEAD, B_CH), lambda bi, j: (bi, 0, 0))],
        out_shape=[jax.ShapeDtypeStruct((b, t, B_CH), F32), jax.ShapeDtypeStruct((b, CONVB_HEAD, B_CH), F32)],
        scratch_shapes=[pltpu.VMEM((CONVB_HEAD + tt, B_CH), F32)],
        compiler_params=_cparams(("arbitrary", "arbitrary")),
        name="conv_module",
    )(proj, proj, init, w, dwb, lng, lnb)


GDN_HEAD = 8


def _unit_lower_inverse(a):
    n = a.shape[0]
    r = lax.broadcasted_iota(jnp.int32, (n, n), 0)
    c = lax.broadcasted_iota(jnp.int32, (n, n), 1)
    x = -a
    p = jnp.where(r == c, 1.0, 0.0) + x
    for _ in range(int(math.log2(n)) - 1):
        x = _dot(x, x, HIGHEST)
        p = p + _dot(p, x, HIGHEST)
    return p


def _gdn_kernel(x_ref, z_ref, misc_ref, cinit_ref, s0_ref, cw_ref, alog_ref, dtb_ref, ng_ref,
                o_ref, tail_ref, sout_ref, f_ref, s_ref):
    cc = x_ref.shape[1]
    c = pl.program_id(1)

    @pl.when(c == 0)
    def _():
        f_ref[0:GDN_HEAD, :] = cinit_ref[0]
        s_ref[...] = s0_ref[0]

    u = x_ref[0]
    f_ref[GDN_HEAD:GDN_HEAD + cc, :] = u
    lead = GDN_HEAD - (D_CONV - 1)
    y = None
    for j in range(D_CONV):
        term = cw_ref[j:j + 1, :] * f_ref[lead + j:lead + j + cc, :]
        y = term if y is None else y + term
    y = _silu(y)
    last = u[cc - GDN_HEAD:cc, :]
    tail_ref[0] = last
    f_ref[0:GDN_HEAD, :] = last

    misc = misc_ref[0]
    beta_all = jax.nn.sigmoid(misc)
    xg = misc + dtb_ref[...]
    softplus = jnp.maximum(xg, 0.0) + jnp.log(1.0 + jnp.exp(-jnp.abs(xg)))
    g_all = -jnp.exp(alog_ref[...]) * softplus
    r = lax.broadcasted_iota(jnp.int32, (cc, cc), 0)
    col = lax.broadcasted_iota(jnp.int32, (cc, cc), 1)
    incl = col <= r
    strict = col < r
    gcum = _dot(jnp.where(incl, 1.0, 0.0), g_all, HIGHEST)
    gcum_t = jnp.transpose(gcum)
    z = z_ref[0]

    for h in range(D_HEADS):
        qh = y[:, h * D_KDIM:(h + 1) * D_KDIM]
        kh = y[:, D_QK + h * D_KDIM:D_QK + (h + 1) * D_KDIM]
        vh = y[:, 2 * D_QK + h * D_VDIM:2 * D_QK + (h + 1) * D_VDIM]
        qn = qh * lax.rsqrt(jnp.sum(qh * qh, axis=-1, keepdims=True) + EPS) * (D_KDIM ** -0.5)
        kn = kh * lax.rsqrt(jnp.sum(kh * kh, axis=-1, keepdims=True) + EPS)
        bc = beta_all[:, MISC_DB + h:MISC_DB + h + 1]
        gc = gcum[:, MISC_DA + h:MISC_DA + h + 1]
        gr = gcum_t[MISC_DA + h:MISC_DA + h + 1, :]
        gl = gcum[cc - 1:cc, MISC_DA + h:MISC_DA + h + 1]
        decay = jnp.where(incl, jnp.exp(jnp.where(incl, gc - gr, 0.0)), 0.0)
        kb = kn.astype(BF16)
        qb = qn.astype(BF16)
        a = jnp.where(strict, bc * decay * _dot_nt(kb, kb), 0.0)
        tinv = _unit_lower_inverse(a)
        s = s_ref[h]
        sb = s.astype(BF16)
        eg = jnp.exp(gc)
        rhs = bc * (vh - eg * _dot(kb, sb))
        uu = _dot(tinv, rhs, HIGHEST)
        ub = uu.astype(BF16)
        qk = _dot_nt(qb, kb) * decay
        o = eg * _dot(qb, sb) + _dot(qk.astype(BF16), ub)
        kd = (kn * jnp.exp(gl - gc)).astype(BF16)
        s_ref[h] = jnp.exp(gl) * s + _dot_tn(kd, ub)
        zh = z[:, h * D_VDIM:(h + 1) * D_VDIM]
        o_ref[0, :, h * D_VDIM:(h + 1) * D_VDIM] = _rms(o, ng_ref[...]) * _silu(zh)

    @pl.when(c == pl.num_programs(1) - 1)
    def _():
        sout_ref[0] = s_ref[...]


def gated_delta(proj, cinit, s0, cw, alog, dtb, ng):
    b, t, _ = proj.shape
    cc = min(t, CHUNK)
    vec = pl.BlockSpec((1, LANES), lambda bi, c: (0, 0))
    return pl.pallas_call(
        _gdn_kernel,
        grid=(b, t // cc),
        in_specs=[pl.BlockSpec((1, cc, D_CONV_CH), lambda bi, c: (bi, c, COL_DQKV // D_CONV_CH)),
                  pl.BlockSpec((1, cc, D_V), lambda bi, c: (bi, c, COL_DZ // D_V)),
                  pl.BlockSpec((1, cc, LANES), lambda bi, c: (bi, c, COL_MISC // LANES)),
                  pl.BlockSpec((1, GDN_HEAD, D_CONV_CH), lambda bi, c: (bi, 0, 0)),
                  pl.BlockSpec((1, D_HEADS, D_KDIM, D_VDIM), lambda bi, c: (bi, 0, 0, 0)),
                  pl.BlockSpec((D_CONV, D_CONV_CH), lambda bi, c: (0, 0)),
                  vec, vec, vec],
        out_specs=[pl.BlockSpec((1, cc, D_V), lambda bi, c: (bi, c, 0)),
                   pl.BlockSpec((1, GDN_HEAD, D_CONV_CH), lambda bi, c: (bi, 0, 0)),
                   pl.BlockSpec((1, D_HEADS, D_KDIM, D_VDIM), lambda bi, c: (bi, 0, 0, 0))],
        out_shape=[jax.ShapeDtypeStruct((b, t, D_V), F32),
                   jax.ShapeDtypeStruct((b, GDN_HEAD, D_CONV_CH), F32),
                   jax.ShapeDtypeStruct((b, D_HEADS, D_KDIM, D_VDIM), F32)],
        scratch_shapes=[pltpu.VMEM((GDN_HEAD + cc, D_CONV_CH), F32), pltpu.VMEM((D_HEADS, D_KDIM, D_VDIM), F32)],
        compiler_params=_cparams(("arbitrary", "arbitrary")),
        name="gated_delta",
    )(proj, proj, proj, cinit, s0, cw, alog, dtb, ng)


def _merge_kernel(oa_ref, ob_ref, oc_ref, od_ref, g0_ref, g1_ref, g2_ref, g3_ref, x_ref, gt_ref, ng_ref,
                  wbr_ref, wout_ref, o_ref):
    bt, tt, d = x_ref.shape
    rows = bt * tt
    merged = None
    for m, (br, gate) in enumerate(((oa_ref, g0_ref), (ob_ref, g1_ref), (oc_ref, g2_ref), (od_ref, g3_ref))):
        term = jax.nn.sigmoid(gate[...].reshape(rows, d)) * _dot(br[...].reshape(rows, BRANCH_W).astype(BF16), wbr_ref[m])
        merged = term if merged is None else merged + term
    mix = _dot(merged.astype(BF16), wout_ref[...])
    o_ref[...] = x_ref[...] + gt_ref[...] * _rms(mix, ng_ref[...]).reshape(bt, tt, d)


def merge_out(branches, proj, x, mod, ng, wbr, wout, bt, tt):
    b, t, d = x.shape
    gb = COL_GATE // d
    br_spec = pl.BlockSpec((bt, tt, BRANCH_W), lambda i, j: (i, j, 0))
    gate_specs = [pl.BlockSpec((bt, tt, d), functools.partial(lambda i, j, m: (i, j, gb + m), m=m))
                  for m in range(N_BRANCH)]
    return pl.pallas_call(
        _merge_kernel,
        grid=(b // bt, t // tt),
        in_specs=[br_spec] * 4 + gate_specs + [
            pl.BlockSpec((bt, tt, d), lambda i, j: (i, j, 0)),
            pl.BlockSpec((bt, 1, d), lambda i, j: (i, 0, 2)),
            pl.BlockSpec((1, d), lambda i, j: (0, 0)),
            pl.BlockSpec((N_BRANCH, BRANCH_W, d), lambda i, j: (0, 0, 0)),
            pl.BlockSpec((d, d), lambda i, j: (0, 0))],
        out_specs=pl.BlockSpec((bt, tt, d), lambda i, j: (i, j, 0)),
        out_shape=jax.ShapeDtypeStruct((b, t, d), F32),
        compiler_params=_cparams(("arbitrary", "arbitrary")),
        name="merge_out",
    )(*branches, proj, proj, proj, proj, x, mod, ng, wbr, wout)


def _mlp_kernel(x_ref, sh_ref, sc_ref, gt_ref, g2_ref, g3_ref, w1_ref, w2_ref, o_ref, h_ref, acc_ref):
    bt, tt, d = x_ref.shape
    f = pl.program_id(2)

    @pl.when(f == 0)
    def _():
        h = _rms(x_ref[...], g2_ref[...]) * (1.0 + sc_ref[...]) + sh_ref[...]
        h_ref[...] = h.reshape(bt * tt, d).astype(BF16)
        acc_ref[...] = jnp.zeros(acc_ref.shape, F32)

    a = jnp.maximum(_dot(h_ref[...], w1_ref[...]), 0.0)
    acc_ref[...] += _dot((a * a).astype(BF16), w2_ref[...])

    @pl.when(f == pl.num_programs(2) - 1)
    def _():
        o_ref[...] = x_ref[...] + gt_ref[...] * _rms(acc_ref[...], g3_ref[...]).reshape(bt, tt, d)


def mlp(x, mod, g2, g3, w1, w2, bt, tt):
    b, t, d = x.shape
    ff = w1.shape[1]
    tf = 1024
    return pl.pallas_call(
        _mlp_kernel,
        grid=(b // bt, t // tt, ff // tf),
        in_specs=[pl.BlockSpec((bt, tt, d), lambda i, j, f: (i, j, 0)),
                  pl.BlockSpec((bt, 1, d), lambda i, j, f: (i, 0, 3)),
                  pl.BlockSpec((bt, 1, d), lambda i, j, f: (i, 0, 4)),
                  pl.BlockSpec((bt, 1, d), lambda i, j, f: (i, 0, 5)),
                  pl.BlockSpec((1, d), lambda i, j, f: (0, 0)),
                  pl.BlockSpec((1, d), lambda i, j, f: (0, 0)),
                  pl.BlockSpec((d, tf), lambda i, j, f: (0, f)),
                  pl.BlockSpec((tf, d), lambda i, j, f: (f, 0))],
        out_specs=pl.BlockSpec((bt, tt, d), lambda i, j, f: (i, j, 0)),
        out_shape=jax.ShapeDtypeStruct((b, t, d), F32),
        scratch_shapes=[pltpu.VMEM((bt * tt, d), BF16), pltpu.VMEM((bt * tt, d), F32)],
        compiler_params=_cparams(("arbitrary", "arbitrary", "arbitrary")),
        name="mlp",
    )(x, mod, mod, mod, g2, g3, w1, w2)


def _combined_in_weight(w_in_l, w_gate_l):
    offs = np.concatenate([[0], np.cumsum(IN_SIZES)])
    (aq, ak, av, bglu, cq, ck, cv, iq, ik, iw, dqkv, dz, db, da) = [
        w_in_l[:, int(offs[i]):int(offs[i + 1])] for i in range(len(IN_SIZES))]
    d = w_in_l.shape[0]
    misc = jnp.concatenate([ik, iw, db, da, jnp.zeros((d, LANES - IDX_DIM - IDX_HEADS - 2 * D_HEADS), F32)], axis=1)
    pad = jnp.zeros((d, COL_GATE - COL_MISC - LANES), F32)
    gates = [w_gate_l[m] for m in range(N_BRANCH)]
    w = jnp.concatenate([aq, ak, av, bglu, cq, ck, cv, dqkv, dz, iq, misc, pad] + gates, axis=1)
    assert w.shape[1] == N_PROJ
    return w.astype(BF16)


def _lane_pad(v, offset):
    return jnp.zeros((1, LANES), F32).at[0, offset:offset + v.shape[0]].set(v)


def _run_group(x, mod, lw, past, bias, rel_bias, tiles):
    b, t, d = x.shape
    bt, tt, conv_tt = tiles
    proj = inproj(x, mod, lw['g'][0:1], lw['w_all'], bt, tt)

    if past is None:
        b_init = jnp.zeros((b, CONVB_HEAD, B_CH), F32)
        d_init = jnp.zeros((b, GDN_HEAD, D_CONV_CH), F32)
        s0 = jnp.zeros((b, D_HEADS, D_KDIM, D_VDIM), F32)
        o_a = flash_causal(proj, bias, rel_bias, lw['lam_init'], diff=True, qcol=COL_AQ, kcol=COL_AK, vcol=COL_AV,
                           bias_col0=0, a_lambda=lw['a_lambda'], a_norm_g=lw['a_norm_g'])
        sel = select_causal(proj)
        o_c = flash_causal(proj, bias, rel_bias, lw['lam_init'], diff=False, qcol=COL_CQ, kcol=COL_CK, vcol=COL_CV,
                           bias_col0=A_HEADS, mask=sel)
    else:
        layer = past['layer']
        lead_b = CONVB_HEAD - (B_WIDTH - 1)
        b_init = jnp.pad(past['b_conv'][layer], ((0, 0), (lead_b, 0), (0, 0)))
        d_init = jnp.pad(past['d_conv'][layer], ((0, 0), (GDN_HEAD - (D_CONV - 1), 0), (0, 0)))
        s0 = past['d_state'][layer]
        o_a = flash_full(proj, past['a_k'], past['a_v'], layer, bias, lw['lam_init'], diff=True,
                         qcol=COL_AQ, kcol=COL_AK, vcol=COL_AV, bias_col0=0,
                         a_lambda=lw['a_lambda'], a_norm_g=lw['a_norm_g'])
        sel = select_full(proj, past['c_kidx'], layer)
        o_c = flash_full(proj, past['c_k'], past['c_v'], layer, bias, lw['lam_init'], diff=False,
                         qcol=COL_CQ, kcol=COL_CK, vcol=COL_CV, bias_col0=A_HEADS, mask=sel)

    o_b, b_tail = conv_module(proj, b_init, lw['b_dw_w'], lw['b_dw_b'], lw['b_ln_g'], lw['b_ln_b'], conv_tt)
    o_d, d_tail, s_new = gated_delta(proj, d_init, s0, lw['d_conv_w'], lw['alog'], lw['dtb'], lw['d_norm_g'])

    mbt, mtt = (1, 256) if t >= 256 else (min(b, 256 // t), t)
    x1 = merge_out((o_a, o_b, o_c, o_d), proj, x, mod, lw['g'][1:2], lw['w_br'], lw['w_out'], mbt, mtt)
    fbt, ftt = (1, 512) if t >= 512 else (min(b, 512 // t), t)
    x2 = mlp(x1, mod, lw['g'][2:3], lw['g'][3:4], lw['w1'], lw['w2'], fbt, ftt)

    new = {
        'a_k': proj[:, :, COL_AK:COL_AK + A_QK].reshape(b, t, A_HEADS, 2 * A_DIM),
        'a_v': proj[:, :, COL_AV:COL_AV + A_QK].reshape(b, t, A_HEADS, 2 * A_DIM),
        'c_k': proj[:, :, COL_CK:COL_CK + C_W].reshape(b, t, C_HEADS, C_DIM),
        'c_v': proj[:, :, COL_CV:COL_CV + C_W].reshape(b, t, C_HEADS, C_DIM),
        'c_kidx': proj[:, :, COL_MISC:COL_MISC + IDX_DIM],
        'b_conv': b_tail[:, CONVB_HEAD - (B_WIDTH - 1):, :],
        'd_conv': d_tail[:, GDN_HEAD - (D_CONV - 1):, :],
        'd_state': s_new,
    }
    return x2, new


def kernel(x_prompt, x_sample, c_prompt, c_sample, cache_a_k, cache_a_v, cache_c_k, cache_c_v, cache_c_kidx, state_b_conv, state_d_conv, state_d_state, rel_bias, ada_w, ada_b, norm_g, w_in, a_lambda, a_norm_g, b_dw_w, b_dw_b, b_ln_g, b_ln_b, d_conv_w, d_a_log, d_dt_bias, d_norm_g, w_gate, w_br, w_out, mlp_w1, mlp_w2):
    depth = w_in.shape[0]
    bp, tp, d = x_prompt.shape
    bs, ts, _ = x_sample.shape
    past_len = cache_a_k.shape[2]

    mod_all = adaln_mod(jnp.concatenate([c_prompt, c_sample], axis=0), ada_w, ada_b)
    bias_p = bias_tiles(rel_bias, FLASH_T, 2 * FLASH_T, FLASH_T, 0)
    bias_s = bias_tiles(rel_bias, ts, past_len + ts, past_len, 0)

    past = {
        'a_k': cache_a_k.reshape(depth, bs, past_len, A_QK),
        'a_v': cache_a_v.reshape(depth, bs, past_len, A_QK),
        'c_k': cache_c_k.reshape(depth, bs, past_len, C_W),
        'c_v': cache_c_v.reshape(depth, bs, past_len, C_W),
        'c_kidx': cache_c_kidx, 'b_conv': state_b_conv, 'd_conv': state_d_conv, 'd_state': state_d_state,
    }
    names = ('a_k', 'a_v', 'c_k', 'c_v', 'c_kidx', 'b_conv', 'd_conv', 'd_state')
    new_p = {n: [] for n in names}
    new_s = {n: [] for n in names}
    xp, xs = x_prompt, x_sample
    for l in range(depth):
        lw = {
            'g': norm_g[l],
            'w_all': _combined_in_weight(w_in[l], w_gate[l]),
            'lam_init': jnp.full((1,), 0.8 - 0.6 * math.exp(-0.3 * l), F32),
            'a_lambda': a_lambda[l],
            'a_norm_g': a_norm_g[l].reshape(1, LANES),
            'b_dw_w': b_dw_w[l], 'b_dw_b': b_dw_b[l].reshape(1, B_CH),
            'b_ln_g': b_ln_g[l].reshape(1, B_CH), 'b_ln_b': b_ln_b[l].reshape(1, B_CH),
            'd_conv_w': d_conv_w[l],
            'alog': _lane_pad(d_a_log[l], MISC_DA), 'dtb': _lane_pad(d_dt_bias[l], MISC_DA),
            'd_norm_g': d_norm_g[l].reshape(1, LANES),
            'w_br': w_br[l].astype(BF16), 'w_out': w_out[l].astype(BF16),
            'w1': mlp_w1[l].astype(BF16), 'w2': mlp_w2[l].astype(BF16),
        }
        mod_p = mod_all[l, :bp].reshape(bp, 1, 6 * d)
        mod_s = mod_all[l, bp:].reshape(bs, 1, 6 * d)
        xp, sp = _run_group(xp, mod_p, lw, None, bias_p, rel_bias, (1, 1024, 256))
        xs, ss = _run_group(xs, mod_s, lw, dict(past, layer=l), bias_s, rel_bias, (min(bs, 1024 // ts), ts, ts))
        for n in names:
            new_p[n].append(sp[n])
            new_s[n].append(ss[n])
    return (xp, xs) + tuple(jnp.stack(new_p[n]) for n in names) + tuple(jnp.stack(new_s[n]) for n in names)
```

```python
import functools
import math

import numpy as np
import jax
import jax.numpy as jnp
from jax import lax
from jax.experimental import pallas as pl
from jax.experimental.pallas import tpu as pltpu

F32 = jnp.float32
BF16 = jnp.bfloat16
HIGHEST = lax.Precision.HIGHEST

D_MODEL = 1024
CHUNK = 64
BRANCH_W = D_MODEL // 2
N_BRANCH = 4
A_HEADS = 4
A_DIM = 64
B_CH = BRANCH_W
B_WIDTH = 31
C_HEADS = 8
C_DIM = 64
IDX_HEADS = 4
IDX_DIM = 64
TOPK_MAX = 256
D_HEADS = 4
D_VDIM = 128
D_KDIM = 64
D_CONV = 4
NUM_BUCKETS = 32
MAX_DISTANCE = 128
D_FF = 4 * D_MODEL
EPS = 1e-6
A_QK = A_HEADS * 2 * A_DIM
C_W = C_HEADS * C_DIM
D_QK = D_HEADS * D_KDIM
D_V = D_HEADS * D_VDIM
D_CONV_CH = 2 * D_QK + D_V
IN_SIZES = (A_QK, A_QK, A_QK, 2 * B_CH, C_W, C_W, C_W, IDX_HEADS * IDX_DIM, IDX_DIM, IDX_HEADS,
            D_CONV_CH, D_V, D_HEADS, D_HEADS)

LANES = 128
NEG = -1e30
LOG2E = 1.4426950408889634
INT_MIN = -2 ** 31
VMEM_LIMIT = 56 * 1024 * 1024

COL_AQ, COL_AK, COL_AV = 0, 512, 1024
COL_BGLU = 1536
COL_CQ, COL_CK, COL_CV = 2560, 3072, 3584
COL_DQKV = 4096
COL_DZ = 5120
COL_IQ = 5632
COL_MISC = 5888
COL_GATE = 6144
N_PROJ = 10240
MISC_IW, MISC_DB, MISC_DA = 64, 68, 72

FLASH_T = 256
FLASH_FAR = 1024
SEL_TQ = 128
SEL_KB = 256
I16_MIN = -2 ** 15


def _cparams(sem):
    return pltpu.CompilerParams(dimension_semantics=sem, vmem_limit_bytes=VMEM_LIMIT)


def _dot(a, b, precision=None):
    return jnp.dot(a, b, preferred_element_type=F32, precision=precision)


def _dot_nt(a, b):
    return lax.dot_general(a, b, (((1,), (1,)), ((), ())), preferred_element_type=F32)


def _dot_tn(a, b):
    return lax.dot_general(a, b, (((0,), (0,)), ((), ())), preferred_element_type=F32)


def _rms(x, g):
    return x * lax.rsqrt(jnp.mean(x * x, axis=-1, keepdims=True) + EPS) * g


def _silu(x):
    return x * jax.nn.sigmoid(x)


def _chunk_of(pos):
    return jnp.right_shift(pos, int(math.log2(CHUNK)))


def _mod_kernel(c_ref, w_ref, b_ref, o_ref):
    s = _silu(c_ref[...])
    o_ref[0] = _dot(s.astype(BF16), w_ref[0].astype(BF16)) + b_ref[0]


def adaln_mod(c_all, ada_w, ada_b):
    depth, d, n = ada_w.shape
    bc = c_all.shape[0]
    tn = 1024
    return pl.pallas_call(
        _mod_kernel,
        grid=(depth, n // tn),
        in_specs=[pl.BlockSpec((bc, d), lambda l, j: (0, 0)),
                  pl.BlockSpec((1, d, tn), lambda l, j: (l, 0, j)),
                  pl.BlockSpec((1, 1, tn), lambda l, j: (l, 0, j))],
        out_specs=pl.BlockSpec((1, bc, tn), lambda l, j: (l, 0, j)),
        out_shape=jax.ShapeDtypeStruct((depth, bc, n), F32),
        compiler_params=_cparams(("arbitrary", "arbitrary")),
        name="adaln_mod",
    )(c_all, ada_w, ada_b.reshape(depth, 1, n))


def _bias_kernel(tab_ref, o_ref, *, q0, k0):
    h = pl.program_id(0)
    tq, w = o_ref.shape[1], o_ref.shape[2]
    row = lax.broadcasted_iota(jnp.int32, (tq, w), 0)
    col = lax.broadcasted_iota(jnp.int32, (tq, w), 1)
    rel = (col + k0) - (row + q0)
    nb = NUM_BUCKETS // 2
    max_exact = nb // 2
    n = jnp.abs(rel)
    large = max_exact + (jnp.log(jnp.maximum(n, max_exact).astype(F32) / max_exact)
                         / math.log(MAX_DISTANCE / max_exact) * (nb - max_exact)).astype(jnp.int32)
    large = jnp.minimum(large, nb - 1)
    bucket = jnp.where(rel > 0, nb, 0) + jnp.where(n < max_exact, n, large)
    val = jnp.zeros((tq, w), F32)
    for bk in range(NUM_BUCKETS):
        val = jnp.where(bucket == bk, tab_ref[bk, h], val)
    o_ref[0] = val * LOG2E


def bias_tiles(rel_bias, tq, w, q0, k0):
    nh = rel_bias.shape[1]
    return pl.pallas_call(
        functools.partial(_bias_kernel, q0=q0, k0=k0),
        grid=(nh,),
        in_specs=[pl.BlockSpec(memory_space=pltpu.SMEM)],
        out_specs=pl.BlockSpec((1, tq, w), lambda h: (h, 0, 0)),
        out_shape=jax.ShapeDtypeStruct((nh, tq, w), F32),
        compiler_params=_cparams(("arbitrary",)),
        name="bias_tiles",
    )(rel_bias)


def _inproj_kernel(x_ref, sh_ref, sc_ref, g_ref, w_ref, o_ref, h_ref):
    bt, tt, d = x_ref.shape

    @pl.when(pl.program_id(2) == 0)
    def _():
        h = _rms(x_ref[...], g_ref[...]) * (1.0 + sc_ref[...]) + sh_ref[...]
        h_ref[...] = h.reshape(bt * tt, d).astype(BF16)

    o_ref[...] = _dot(h_ref[...], w_ref[...]).reshape(o_ref.shape)


def inproj(x, mod, g, w_all, bt, tt):
    b, t, d = x.shape
    n = w_all.shape[1]
    tn = 1024
    return pl.pallas_call(
        _inproj_kernel,
        grid=(b // bt, t // tt, n // tn),
        in_specs=[pl.BlockSpec((bt, tt, d), lambda i, j, k: (i, j, 0)),
                  pl.BlockSpec((bt, 1, d), lambda i, j, k: (i, 0, 0)),
                  pl.BlockSpec((bt, 1, d), lambda i, j, k: (i, 0, 1)),
                  pl.BlockSpec((1, d), lambda i, j, k: (0, 0)),
                  pl.BlockSpec((d, tn), lambda i, j, k: (0, k))],
        out_specs=pl.BlockSpec((bt, tt, tn), lambda i, j, k: (i, j, k)),
        out_shape=jax.ShapeDtypeStruct((b, t, n), F32),
        scratch_shapes=[pltpu.VMEM((bt * tt, d), BF16)],
        compiler_params=_cparams(("arbitrary", "arbitrary", "arbitrary")),
        name="inproj",
    )(x, mod, mod, g, w_all)


def _flash_update(m, qm, kb, vb, bias, shift, add_mask, keep, m_ref, l_ref, acc_ref):
    s = _dot_nt(qm, kb)
    if bias is not None:
        s = s + bias
    if add_mask is not None:
        s = s + add_mask
    if keep is not None:
        s = jnp.where(keep, s, NEG)
    m_prev = m_ref[m]
    s_max = jnp.max(s, axis=-1, keepdims=True)
    if shift is not None:
        s_max = s_max + shift
    m_new = jnp.maximum(m_prev, s_max)
    alpha = jnp.exp2(m_prev - m_new)
    p = jnp.exp2(s - (m_new if shift is None else m_new - shift))
    l_ref[m] = alpha * l_ref[m] + jnp.sum(p, axis=-1, keepdims=True)
    acc_ref[m] = alpha * acc_ref[m] + _dot(p.astype(BF16), vb)
    m_ref[m] = m_new


def _flash_finish(diff, lam_ref, alam_ref, g_ref, o_ref, l_ref, acc_ref, lane):
    o0 = acc_ref[0] / l_ref[0]
    o1 = acc_ref[1] / l_ref[1]
    if diff:
        lv = alam_ref[...]
        lam_init = lam_ref[0]
        lam = (jnp.exp(jnp.sum(lv[0:1] * lv[1:2], axis=-1, keepdims=True))
               - jnp.exp(jnp.sum(lv[2:3] * lv[3:4], axis=-1, keepdims=True)) + lam_init)
        o = o0 - lam * o1
        o = _rms(o, g_ref[...]) * (1.0 - lam_init)
    else:
        o = jnp.where(lane < C_DIM, o0, o1)
    o_ref[0] = o


def _flash_init(q_ref, m_ref, l_ref, acc_ref):
    tq = q_ref.shape[1]
    lane = lax.broadcasted_iota(jnp.int32, (1, LANES), 1)
    q = q_ref[0] * (A_DIM ** -0.5 * LOG2E)
    q0 = jnp.where(lane < A_DIM, q, 0.0).astype(BF16)
    q1 = jnp.where(lane >= A_DIM, q, 0.0).astype(BF16)
    m_ref[...] = jnp.full(m_ref.shape, NEG, F32)
    l_ref[...] = jnp.zeros(l_ref.shape, F32)
    acc_ref[...] = jnp.zeros(acc_ref.shape, F32)
    return lane, (q0, q1)


def _flash_causal_kernel(*refs, diff, has_mask, col0):
    tab_ref, lam_ref, q_ref, k_ref, v_ref, bias_ref = refs[:6]
    pos = 6
    mask_ref = None
    if has_mask:
        mask_ref = refs[pos]
        pos += 1
    alam_ref = g_ref = None
    if diff:
        alam_ref, g_ref = refs[pos], refs[pos + 1]
        pos += 2
    o_ref, m_ref, l_ref, acc_ref = refs[pos:pos + 4]
    t = q_ref.shape[1]
    nb = bias_ref.shape[0]
    p = pl.program_id(1)
    i = pl.program_id(2)
    lane, qs = _flash_init(q_ref, m_ref, l_ref, acc_ref)
    far = [tab_ref[NUM_BUCKETS // 2 - 1, col0 + nb * p + min(m, nb - 1)] * LOG2E for m in range(2)]

    def tile(ks, width, biases, shifts, keep):
        kb = k_ref[0, pl.ds(ks, width), :].astype(BF16)
        vb = v_ref[0, pl.ds(ks, width), :].astype(BF16)
        add = mask_ref[0, :, pl.ds(ks, width)].astype(F32) if has_mask else None
        for m in range(2):
            _flash_update(m, qs[m], kb, vb, biases[m], shifts[m], add, keep, m_ref, l_ref, acc_ref)

    n_far = jnp.maximum(i - 1, 0)
    wide = FLASH_FAR // t
    n_wide = n_far // wide

    def wide_body(kj, carry):
        tile(pl.multiple_of(kj * FLASH_FAR, FLASH_FAR), FLASH_FAR, (None, None), far, None)
        return carry

    def narrow_body(kj, carry):
        tile(pl.multiple_of(kj * t, t), t, (None, None), far, None)
        return carry

    lax.fori_loop(0, n_wide, wide_body, 0)
    lax.fori_loop(n_wide * wide, n_far, narrow_body, 0)

    row = lax.broadcasted_iota(jnp.int32, (t, 2 * t), 0)
    col = lax.broadcasted_iota(jnp.int32, (t, 2 * t), 1)
    keep = _chunk_of(col) <= _chunk_of(row) + t // CHUNK

    @pl.when(i == 0)
    def _():
        tile(0, t, [bias_ref[min(m, nb - 1), :, t:2 * t] for m in range(2)], (None, None), keep[:, t:2 * t])

    @pl.when(i >= 1)
    def _():
        tile(pl.multiple_of((i - 1) * t, t), 2 * t, [bias_ref[min(m, nb - 1)] for m in range(2)], (None, None), keep)

    _flash_finish(diff, lam_ref, alam_ref, g_ref, o_ref, l_ref, acc_ref, lane)


def flash_causal(proj, near_bias, rel_bias, lam_init, *, diff, qcol, kcol, vcol, bias_col0,
                 mask=None, a_lambda=None, a_norm_g=None):
    b, t_all, _ = proj.shape
    t = FLASH_T
    npair = 4
    nb = 1 if diff else 2
    qb, kb, vb = qcol // LANES, kcol // LANES, vcol // LANES
    bb0 = bias_col0 // nb
    in_specs = [pl.BlockSpec(memory_space=pltpu.SMEM),
                pl.BlockSpec(memory_space=pltpu.SMEM),
                pl.BlockSpec((1, t, LANES), lambda bi, p, i: (bi, i, qb + p)),
                pl.BlockSpec((1, t_all, LANES), lambda bi, p, i: (bi, 0, kb + p)),
                pl.BlockSpec((1, t_all, LANES), lambda bi, p, i: (bi, 0, vb + p)),
                pl.BlockSpec((nb, t, 2 * t), lambda bi, p, i: (bb0 + p, 0, 0))]
    args = [rel_bias, lam_init, proj, proj, proj, near_bias]
    if mask is not None:
        in_specs.append(pl.BlockSpec((1, t, t_all), lambda bi, p, i: (bi, i, 0)))
        args.append(mask)
    if diff:
        in_specs += [pl.BlockSpec((4, A_DIM), lambda bi, p, i: (0, 0)),
                     pl.BlockSpec((1, LANES), lambda bi, p, i: (0, 0))]
        args += [a_lambda, a_norm_g]
    return pl.pallas_call(
        functools.partial(_flash_causal_kernel, diff=diff, has_mask=mask is not None, col0=bias_col0),
        grid=(b, npair, t_all // t),
        in_specs=in_specs,
        out_specs=pl.BlockSpec((1, t, LANES), lambda bi, p, i: (bi, i, p)),
        out_shape=jax.ShapeDtypeStruct((b, t_all, npair * LANES), F32),
        scratch_shapes=[pltpu.VMEM((2, t, 1), F32), pltpu.VMEM((2, t, 1), F32), pltpu.VMEM((2, t, LANES), F32)],
        compiler_params=_cparams(("arbitrary", "arbitrary", "arbitrary")),
        name="flash_diff" if diff else "flash_sel",
    )(*args)


def _flash_full_kernel(*refs, diff, has_mask, tk):
    lam_ref, q_ref, kc_ref, vc_ref, kn_ref, vn_ref, bias_ref = refs[:7]
    pos = 7
    mask_ref = None
    if has_mask:
        mask_ref = refs[pos]
        pos += 1
    alam_ref = g_ref = None
    if diff:
        alam_ref, g_ref = refs[pos], refs[pos + 1]
        pos += 2
    o_ref, m_ref, l_ref, acc_ref = refs[pos:pos + 4]
    past = kc_ref.shape[2]
    tn = kn_ref.shape[1]
    nb = bias_ref.shape[0]
    lane, qs = _flash_init(q_ref, m_ref, l_ref, acc_ref)

    def tile(kb, vb, c0, width):
        add = mask_ref[0, :, c0:c0 + width].astype(F32) if has_mask else None
        for m in range(2):
            _flash_update(m, qs[m], kb, vb, bias_ref[min(m, nb - 1), :, c0:c0 + width], None, add, None,
                          m_ref, l_ref, acc_ref)

    for j in range(past // tk):
        tile(kc_ref[0, 0, j * tk:(j + 1) * tk, :].astype(BF16), vc_ref[0, 0, j * tk:(j + 1) * tk, :].astype(BF16),
             j * tk, tk)
    tile(kn_ref[0].astype(BF16), vn_ref[0].astype(BF16), past, tn)
    _flash_finish(diff, lam_ref, alam_ref, g_ref, o_ref, l_ref, acc_ref, lane)


def flash_full(proj, k_cache, v_cache, layer, full_bias, lam_init, *, diff, qcol, kcol, vcol, bias_col0,
               mask=None, a_lambda=None, a_norm_g=None):
    b, tq, _ = proj.shape
    past = k_cache.shape[2]
    npair = 4
    nb = 1 if diff else 2
    ltot = past + tq
    qb, kb, vb = qcol // LANES, kcol // LANES, vcol // LANES
    bb0 = bias_col0 // nb
    in_specs = [pl.BlockSpec(memory_space=pltpu.SMEM),
                pl.BlockSpec((1, tq, LANES), lambda bi, p: (bi, 0, qb + p)),
                pl.BlockSpec((1, 1, past, LANES), lambda bi, p: (layer, bi, 0, p)),
                pl.BlockSpec((1, 1, past, LANES), lambda bi, p: (layer, bi, 0, p)),
                pl.BlockSpec((1, tq, LANES), lambda bi, p: (bi, 0, kb + p)),
                pl.BlockSpec((1, tq, LANES), lambda bi, p: (bi, 0, vb + p)),
                pl.BlockSpec((nb, tq, ltot), lambda bi, p: (bb0 + p, 0, 0))]
    args = [lam_init, proj, k_cache, v_cache, proj, proj, full_bias]
    if mask is not None:
        in_specs.append(pl.BlockSpec((1, tq, mask.shape[2]), lambda bi, p: (bi, 0, 0)))
        args.append(mask)
    if diff:
        in_specs += [pl.BlockSpec((4, A_DIM), lambda bi, p: (0, 0)),
                     pl.BlockSpec((1, LANES), lambda bi, p: (0, 0))]
        args += [a_lambda, a_norm_g]
    return pl.pallas_call(
        functools.partial(_flash_full_kernel, diff=diff, has_mask=mask is not None, tk=256),
        grid=(b, npair),
        in_specs=in_specs,
        out_specs=pl.BlockSpec((1, tq, LANES), lambda bi, p: (bi, 0, p)),
        out_shape=jax.ShapeDtypeStruct((b, tq, npair * LANES), F32),
        scratch_shapes=[pltpu.VMEM((2, tq, 1), F32), pltpu.VMEM((2, tq, 1), F32), pltpu.VMEM((2, tq, LANES), F32)],
        compiler_params=_cparams(("arbitrary", "arbitrary")),
        name="flash_diff_full" if diff else "flash_sel_full",
    )(*args)


def _index_operands(qi, misc, width):
    tq = qi.shape[0]
    w = misc[:, MISC_IW:MISC_IW + IDX_HEADS] * (IDX_HEADS ** -0.5 * IDX_DIM ** -0.5)
    qh = [qi[:, h * IDX_DIM:(h + 1) * IDX_DIM].astype(BF16) for h in range(IDX_HEADS)]
    wh = [jnp.broadcast_to(w[:, h:h + 1], (tq, width)) for h in range(IDX_HEADS)]
    return qh, wh


def _index_scores(qh, wh, kb):
    isc = None
    for q, w in zip(qh, wh):
        term = w[:, :kb.shape[0]] * jnp.maximum(_dot_nt(q, kb), 0.0)
        isc = term if isc is None else isc + term
    return isc + 0.0


def _sortable(x):
    bits = lax.bitcast_convert_type(x, jnp.int32)
    return jnp.where(bits < 0, bits ^ jnp.int32(0x7FFFFFFF), bits)


def _store_keys(keys_ref, hi_ref, cs, key):
    width = key.shape[1]
    keys_ref[:, pl.ds(cs, width)] = key
    hi_ref[:, pl.ds(cs, width)] = jnp.right_shift(key, 16).astype(jnp.int16)


def _select_topk(keys_ref, hi_ref, lo_ref, nkb, k_sel, out_ref):
    tq = keys_ref.shape[0]

    def block(ref, j):
        return ref[:, pl.ds(pl.multiple_of(j * SEL_KB, SEL_KB), SEL_KB)]

    def count_ge(ref, cand):
        cb = jnp.broadcast_to(cand, (tq, SEL_KB)).astype(jnp.int16)

        def body(j, acc):
            return acc + jnp.where(block(ref, j) >= cb, jnp.int16(1), jnp.int16(0))

        acc = lax.fori_loop(0, nkb, body, jnp.zeros((tq, SEL_KB), jnp.int16))
        return jnp.sum(acc.astype(jnp.int32), axis=-1, keepdims=True)

    def count_gt(ref, t):
        top = -I16_MIN - 1
        return jnp.where(t >= top, 0, count_ge(ref, jnp.minimum(t + 1, top)))

    def search(ref, base):
        zero = jnp.zeros((tq, 1), jnp.int32)
        t0 = jnp.where(base + count_ge(ref, zero) >= k_sel, zero, jnp.full((tq, 1), I16_MIN, jnp.int32))

        def bit_body(it, t):
            cand = t + jnp.left_shift(jnp.int32(1), 14 - it)
            return jnp.where(base + count_ge(ref, cand) >= k_sel, cand, t)

        return lax.fori_loop(0, 15, bit_body, t0)

    hi = search(hi_ref, jnp.zeros((tq, 1), jnp.int32))
    above = count_gt(hi_ref, hi)
    hib = jnp.broadcast_to(hi, (tq, SEL_KB))

    def lo_body(j, carry):
        key = block(keys_ref, j)
        lo = jnp.bitwise_and(key, 0xFFFF) + I16_MIN
        lo = jnp.where(jnp.right_shift(key, 16) == hib, lo, I16_MIN)
        lo_ref[:, pl.ds(pl.multiple_of(j * SEL_KB, SEL_KB), SEL_KB)] = lo.astype(jnp.int16)
        return carry

    lax.fori_loop(0, nkb, lo_body, 0)
    lo = search(lo_ref, above)
    thr = hi * 65536 + (lo - I16_MIN)
    need = k_sel - (above + count_gt(lo_ref, lo))
    need = jnp.where(thr == INT_MIN, 0, need).astype(F32)
    thrb = jnp.broadcast_to(thr, (tq, SEL_KB))
    needb = jnp.broadcast_to(need, (tq, SEL_KB))
    r = lax.broadcasted_iota(jnp.int32, (SEL_KB, 2 * SEL_KB), 0)
    c = lax.broadcasted_iota(jnp.int32, (SEL_KB, 2 * SEL_KB), 1)
    tri = jnp.where((r <= c) | (c >= SEL_KB), 1.0, 0.0).astype(BF16)

    def mask_body(j, offs):
        blk = block(keys_ref, j)
        eq = blk == thrb
        cnt = _dot(jnp.where(eq, 1.0, 0.0).astype(BF16), tri)
        sel = (blk > thrb) | (eq & (offs + cnt[:, :SEL_KB] <= needb))
        out_ref[0, :, pl.ds(pl.multiple_of(j * SEL_KB, SEL_KB), SEL_KB)] = jnp.where(sel, 0.0, NEG).astype(out_ref.dtype)
        return offs + cnt[:, SEL_KB:]

    lax.fori_loop(0, nkb, mask_body, jnp.zeros((tq, SEL_KB), F32))


def _select_causal_kernel(iq_ref, mq_ref, mk_ref, o_ref, keys_ref, hi_ref, lo_ref, *, k_sel):
    tq = iq_ref.shape[1]
    i = pl.program_id(1)
    nkb = (i * tq) // SEL_KB + 1
    qh, wh = _index_operands(iq_ref[0], mq_ref[0], SEL_KB)
    row = lax.broadcasted_iota(jnp.int32, (tq, SEL_KB), 0) + i * tq
    col = lax.broadcasted_iota(jnp.int32, (tq, SEL_KB), 1)

    def score_body(j, carry):
        cs = pl.multiple_of(j * SEL_KB, SEL_KB)
        kb = mk_ref[0, pl.ds(cs, SEL_KB), :][:, 0:IDX_DIM].astype(BF16)
        key = _sortable(_index_scores(qh, wh, kb))
        vis = _chunk_of(col + cs) <= _chunk_of(row)
        _store_keys(keys_ref, hi_ref, cs, jnp.where(vis, key, INT_MIN))
        return carry

    lax.fori_loop(0, nkb, score_body, 0)
    _select_topk(keys_ref, hi_ref, lo_ref, nkb, k_sel, o_ref)


def select_causal(proj):
    b, t, _ = proj.shape
    tq = SEL_TQ
    k_sel = min(TOPK_MAX, t // 4)
    return pl.pallas_call(
        functools.partial(_select_causal_kernel, k_sel=k_sel),
        grid=(b, t // tq),
        in_specs=[pl.BlockSpec((1, tq, IDX_HEADS * IDX_DIM), lambda bi, i: (bi, i, COL_IQ // (IDX_HEADS * IDX_DIM))),
                  pl.BlockSpec((1, tq, LANES), lambda bi, i: (bi, i, COL_MISC // LANES)),
                  pl.BlockSpec((1, t, LANES), lambda bi, i: (bi, 0, COL_MISC // LANES))],
        out_specs=pl.BlockSpec((1, tq, t), lambda bi, i: (bi, i, 0)),
        out_shape=jax.ShapeDtypeStruct((b, t, t), BF16),
        scratch_shapes=[pltpu.VMEM((tq, t), jnp.int32), pltpu.VMEM((tq, t), jnp.int16),
                        pltpu.VMEM((tq, t), jnp.int16)],
        compiler_params=_cparams(("arbitrary", "arbitrary")),
        name="select_causal",
    )(proj, proj, proj)


def _select_full_kernel(iq_ref, mq_ref, kc_ref, o_ref, keys_ref, hi_ref, lo_ref, *, k_sel):
    tq = iq_ref.shape[1]
    past = kc_ref.shape[2]
    nkb = keys_ref.shape[1] // SEL_KB
    misc = mq_ref[0]
    qh, wh = _index_operands(iq_ref[0], misc, SEL_KB)
    for j in range(past // SEL_KB):
        kb = kc_ref[0, 0, j * SEL_KB:(j + 1) * SEL_KB, :].astype(BF16)
        _store_keys(keys_ref, hi_ref, j * SEL_KB, _sortable(_index_scores(qh, wh, kb)))
    _store_keys(keys_ref, hi_ref, past, jnp.full((tq, keys_ref.shape[1] - past), INT_MIN, jnp.int32))
    _store_keys(keys_ref, hi_ref, past, _sortable(_index_scores(qh, wh, misc[:, 0:IDX_DIM].astype(BF16))))
    _select_topk(keys_ref, hi_ref, lo_ref, nkb, k_sel, o_ref)


def select_full(proj, kidx_cache, layer):
    b, tq, _ = proj.shape
    past = kidx_cache.shape[2]
    ltot = past + tq
    lpad = -(-ltot // SEL_KB) * SEL_KB
    k_sel = min(TOPK_MAX, ltot // 4)
    return pl.pallas_call(
        functools.partial(_select_full_kernel, k_sel=k_sel),
        grid=(b,),
        in_specs=[pl.BlockSpec((1, tq, IDX_HEADS * IDX_DIM), lambda bi: (bi, 0, COL_IQ // (IDX_HEADS * IDX_DIM))),
                  pl.BlockSpec((1, tq, LANES), lambda bi: (bi, 0, COL_MISC // LANES)),
                  pl.BlockSpec((1, 1, past, IDX_DIM), lambda bi: (layer, bi, 0, 0))],
        out_specs=pl.BlockSpec((1, tq, lpad), lambda bi: (bi, 0, 0)),
        out_shape=jax.ShapeDtypeStruct((b, tq, lpad), BF16),
        scratch_shapes=[pltpu.VMEM((tq, lpad), jnp.int32), pltpu.VMEM((tq, lpad), jnp.int16),
                        pltpu.VMEM((tq, lpad), jnp.int16)],
        compiler_params=_cparams(("arbitrary",)),
        name="select_full",
    )(proj, proj, kidx_cache)


CONVB_ROWS = 64
CONVB_HEAD = 32


def _convb_kernel(a_ref, gt_ref, init_ref, w_ref, dwb_ref, lng_ref, lnb_ref, o_ref, tail_ref, f_ref):
    tt = a_ref.shape[1]
    lead = CONVB_HEAD - (B_WIDTH - 1)

    @pl.when(pl.program_id(1) == 0)
    def _():
        f_ref[0:CONVB_HEAD, :] = init_ref[0]

    f_ref[CONVB_HEAD:CONVB_HEAD + tt, :] = a_ref[0] * jax.nn.sigmoid(gt_ref[0])
    for r in range(tt // CONVB_ROWS):
        acc = None
        for k in range(B_WIDTH):
            s0 = r * CONVB_ROWS + lead + k
            term = w_ref[k:k + 1, :] * f_ref[s0:s0 + CONVB_ROWS, :]
            acc = term if acc is None else acc + term
        y = acc + dwb_ref[...]
        yc = y - jnp.mean(y, axis=-1, keepdims=True)
        yn = yc * lax.rsqrt(jnp.mean(yc * yc, axis=-1, keepdims=True) + EPS) * lng_ref[...] + lnb_ref[...]
        o_ref[0, r * CONVB_ROWS:(r + 1) * CONVB_ROWS, :] = _silu(yn)
    last = f_ref[tt:tt + CONVB_HEAD, :]
    tail_ref[0] = last
    f_ref[0:CONVB_HEAD, :] = last


def conv_module(proj, init, w, dwb, lng, lnb, tt):
    b, t, _ = proj.shape
    cb = COL_BGLU // B_CH
    vec = pl.BlockSpec((1, B_CH), lambda bi, j: (0, 0))
    return pl.pallas_call(
        _convb_kernel,
        grid=(b, t // tt),
        in_specs=[pl.BlockSpec((1, tt, B_CH), lambda bi, j: (bi, j, cb)),
                  pl.BlockSpec((1, tt, B_CH), lambda bi, j: (bi, j, cb + 1)),
                  pl.BlockSpec((1, CONVB_HEAD, B_CH), lambda bi, j: (bi, 0, 0)),
                  pl.BlockSpec((B_WIDTH, B_CH), lambda bi, j: (0, 0)),
                  vec, vec, vec],
        out_specs=[pl.BlockSpec((1, tt, B_CH), lambda bi, j: (bi, j, 0)),
                   pl.BlockSpec((1, CONVB_HEAD, B_CH), lambda bi, j: (bi, 0, 0))],
        out_shape=[jax.ShapeDtypeStruct((b, t, B_CH), F32), jax.ShapeDtypeStruct((b, CONVB_HEAD, B_CH), F32)],
        scratch_shapes=[pltpu.VMEM((CONVB_HEAD + tt, B_CH), F32)],
        compiler_params=_cparams(("arbitrary", "arbitrary")),
        name="conv_module",
    )(proj, proj, init, w, dwb, lng, lnb)


GDN_HEAD = 8
GDN_BB = 4


def _unit_lower_inverses(mats):
    n = mats[0].shape[0]
    r = lax.broadcasted_iota(jnp.int32, (n, n), 0)
    c = lax.broadcasted_iota(jnp.int32, (n, n), 1)
    eye = jnp.where(r == c, 1.0, 0.0)
    xs = [-a for a in mats]
    ps = [eye + x for x in xs]
    for _ in range(int(math.log2(n)) - 1):
        xs = [_dot(x, x, HIGHEST) for x in xs]
        ps = [p + _dot(p, x, HIGHEST) for p, x in zip(ps, xs)]
    return ps


def _gdn_kernel(x_ref, z_ref, misc_ref, cinit_ref, s0_ref, cw_ref, alog_ref, dtb_ref, ng_ref,
                o_ref, tail_ref, sout_ref, f_ref, s_ref):
    c = pl.program_id(1)

    @pl.when(c == 0)
    def _():
        f_ref[:, 0:GDN_HEAD, :] = cinit_ref[...]
        s_ref[...] = s0_ref[...]

    bb, cc = x_ref.shape[0], x_ref.shape[1]
    lead = GDN_HEAD - (D_CONV - 1)
    r = lax.broadcasted_iota(jnp.int32, (cc, cc), 0)
    col = lax.broadcasted_iota(jnp.int32, (cc, cc), 1)
    incl = col <= r
    strict = col < r
    lower = jnp.where(incl, 1.0, 0.0)

    ys, betas, gcums, gcum_ts = [], [], [], []
    for bi in range(bb):
        u = x_ref[bi]
        f_ref[bi, GDN_HEAD:GDN_HEAD + cc, :] = u
        y = None
        for j in range(D_CONV):
            term = cw_ref[j:j + 1, :] * f_ref[bi, lead + j:lead + j + cc, :]
            y = term if y is None else y + term
        ys.append(_silu(y))
        last = u[cc - GDN_HEAD:cc, :]
        tail_ref[bi] = last
        f_ref[bi, 0:GDN_HEAD, :] = last
        misc = misc_ref[bi]
        betas.append(jax.nn.sigmoid(misc))
        xg = misc + dtb_ref[...]
        softplus = jnp.maximum(xg, 0.0) + jnp.log(1.0 + jnp.exp(-jnp.abs(xg)))
        g_all = -jnp.exp(alog_ref[...]) * softplus
        gcum = _dot(lower, g_all, HIGHEST)
        gcums.append(gcum)
        gcum_ts.append(jnp.transpose(gcum))

    prob = [(bi, h) for bi in range(bb) for h in range(D_HEADS)]
    qn, kn, vh, bc, gc, gl, decay = [], [], [], [], [], [], []
    for bi, h in prob:
        y = ys[bi]
        q = y[:, h * D_KDIM:(h + 1) * D_KDIM]
        k = y[:, D_QK + h * D_KDIM:D_QK + (h + 1) * D_KDIM]
        vh.append(y[:, 2 * D_QK + h * D_VDIM:2 * D_QK + (h + 1) * D_VDIM])
        qn.append(q * lax.rsqrt(jnp.sum(q * q, axis=-1, keepdims=True) + EPS) * (D_KDIM ** -0.5))
        kn.append(k * lax.rsqrt(jnp.sum(k * k, axis=-1, keepdims=True) + EPS))
        bc.append(betas[bi][:, MISC_DB + h:MISC_DB + h + 1])
        g_col = gcums[bi][:, MISC_DA + h:MISC_DA + h + 1]
        g_row = gcum_ts[bi][MISC_DA + h:MISC_DA + h + 1, :]
        gc.append(g_col)
        gl.append(gcums[bi][cc - 1:cc, MISC_DA + h:MISC_DA + h + 1])
        decay.append(jnp.where(incl, jnp.exp(jnp.where(incl, g_col - g_row, 0.0)), 0.0))
    n = len(prob)
    kb = [x.astype(BF16) for x in kn]
    qb = [x.astype(BF16) for x in qn]
    kk = [_dot_nt(kb[g], kb[g]) for g in range(n)]
    tinv = _unit_lower_inverses([jnp.where(strict, bc[g] * decay[g] * kk[g], 0.0) for g in range(n)])
    s = [s_ref[bi, h] for bi, h in prob]
    sb = [x.astype(BF16) for x in s]
    eg = [jnp.exp(x) for x in gc]
    ks = [_dot(kb[g], sb[g]) for g in range(n)]
    uu = [_dot(tinv[g], bc[g] * (vh[g] - eg[g] * ks[g]), HIGHEST) for g in range(n)]
    ub = [x.astype(BF16) for x in uu]
    qk = [(_dot_nt(qb[g], kb[g]) * decay[g]).astype(BF16) for g in range(n)]
    qs = [_dot(qb[g], sb[g]) for g in range(n)]
    o = [eg[g] * qs[g] + _dot(qk[g], ub[g]) for g in range(n)]
    kd = [(kn[g] * jnp.exp(gl[g] - gc[g])).astype(BF16) for g in range(n)]
    s_new = [jnp.exp(gl[g]) * s[g] + _dot_tn(kd[g], ub[g]) for g in range(n)]
    for g, (bi, h) in enumerate(prob):
        s_ref[bi, h] = s_new[g]
        zh = z_ref[bi, :, h * D_VDIM:(h + 1) * D_VDIM]
        o_ref[bi, :, h * D_VDIM:(h + 1) * D_VDIM] = _rms(o[g], ng_ref[...]) * _silu(zh)

    @pl.when(c == pl.num_programs(1) - 1)
    def _():
        sout_ref[...] = s_ref[...]


def gated_delta(proj, cinit, s0, cw, alog, dtb, ng):
    b, t, _ = proj.shape
    cc = min(t, CHUNK)
    bb = math.gcd(b, GDN_BB)
    vec = pl.BlockSpec((1, LANES), lambda bi, c: (0, 0))
    return pl.pallas_call(
        _gdn_kernel,
        grid=(b // bb, t // cc),
        in_specs=[pl.BlockSpec((bb, cc, D_CONV_CH), lambda bi, c: (bi, c, COL_DQKV // D_CONV_CH)),
                  pl.BlockSpec((bb, cc, D_V), lambda bi, c: (bi, c, COL_DZ // D_V)),
                  pl.BlockSpec((bb, cc, LANES), lambda bi, c: (bi, c, COL_MISC // LANES)),
                  pl.BlockSpec((bb, GDN_HEAD, D_CONV_CH), lambda bi, c: (bi, 0, 0)),
                  pl.BlockSpec((bb, D_HEADS, D_KDIM, D_VDIM), lambda bi, c: (bi, 0, 0, 0)),
                  pl.BlockSpec((D_CONV, D_CONV_CH), lambda bi, c: (0, 0)),
                  vec, vec, vec],
        out_specs=[pl.BlockSpec((bb, cc, D_V), lambda bi, c: (bi, c, 0)),
                   pl.BlockSpec((bb, GDN_HEAD, D_CONV_CH), lambda bi, c: (bi, 0, 0)),
                   pl.BlockSpec((bb, D_HEADS, D_KDIM, D_VDIM), lambda bi, c: (bi, 0, 0, 0))],
        out_shape=[jax.ShapeDtypeStruct((b, t, D_V), F32),
                   jax.ShapeDtypeStruct((b, GDN_HEAD, D_CONV_CH), F32),
                   jax.ShapeDtypeStruct((b, D_HEADS, D_KDIM, D_VDIM), F32)],
        scratch_shapes=[pltpu.VMEM((bb, GDN_HEAD + cc, D_CONV_CH), F32),
                        pltpu.VMEM((bb, D_HEADS, D_KDIM, D_VDIM), F32)],
        compiler_params=_cparams(("arbitrary", "arbitrary")),
        name="gated_delta",
    )(proj, proj, proj, cinit, s0, cw, alog, dtb, ng)


def _merge_kernel(oa_ref, ob_ref, oc_ref, od_ref, g0_ref, g1_ref, g2_ref, g3_ref, x_ref, gt_ref, ng_ref,
                  wbr_ref, wout_ref, o_ref):
    bt, tt, d = x_ref.shape
    rows = bt * tt
    merged = None
    for m, (br, gate) in enumerate(((oa_ref, g0_ref), (ob_ref, g1_ref), (oc_ref, g2_ref), (od_ref, g3_ref))):
        term = jax.nn.sigmoid(gate[...].reshape(rows, d)) * _dot(br[...].reshape(rows, BRANCH_W).astype(BF16), wbr_ref[m])
        merged = term if merged is None else merged + term
    mix = _dot(merged.astype(BF16), wout_ref[...])
    o_ref[...] = x_ref[...] + gt_ref[...] * _rms(mix, ng_ref[...]).reshape(bt, tt, d)


def merge_out(branches, proj, x, mod, ng, wbr, wout, bt, tt):
    b, t, d = x.shape
    gb = COL_GATE // d
    br_spec = pl.BlockSpec((bt, tt, BRANCH_W), lambda i, j: (i, j, 0))
    gate_specs = [pl.BlockSpec((bt, tt, d), functools.partial(lambda i, j, m: (i, j, gb + m), m=m))
                  for m in range(N_BRANCH)]
    return pl.pallas_call(
        _merge_kernel,
        grid=(b // bt, t // tt),
        in_specs=[br_spec] * 4 + gate_specs + [
            pl.BlockSpec((bt, tt, d), lambda i, j: (i, j, 0)),
            pl.BlockSpec((bt, 1, d), lambda i, j: (i, 0, 2)),
            pl.BlockSpec((1, d), lambda i, j: (0, 0)),
            pl.BlockSpec((N_BRANCH, BRANCH_W, d), lambda i, j: (0, 0, 0)),
            pl.BlockSpec((d, d), lambda i, j: (0, 0))],
        out_specs=pl.BlockSpec((bt, tt, d), lambda i, j: (i, j, 0)),
        out_shape=jax.ShapeDtypeStruct((b, t, d), F32),
        compiler_params=_cparams(("arbitrary", "arbitrary")),
        name="merge_out",
    )(*branches, proj, proj, proj, proj, x, mod, ng, wbr, wout)


def _mlp_kernel(x_ref, sh_ref, sc_ref, gt_ref, g2_ref, g3_ref, w1_ref, w2_ref, o_ref, h_ref, acc_ref):
    bt, tt, d = x_ref.shape
    f = pl.program_id(2)

    @pl.when(f == 0)
    def _():
        h = _rms(x_ref[...], g2_ref[...]) * (1.0 + sc_ref[...]) + sh_ref[...]
        h_ref[...] = h.reshape(bt * tt, d).astype(BF16)
        acc_ref[...] = jnp.zeros(acc_ref.shape, F32)

    a = jnp.maximum(_dot(h_ref[...], w1_ref[...]), 0.0)
    acc_ref[...] += _dot((a * a).astype(BF16), w2_ref[...])

    @pl.when(f == pl.num_programs(2) - 1)
    def _():
        o_ref[...] = x_ref[...] + gt_ref[...] * _rms(acc_ref[...], g3_ref[...]).reshape(bt, tt, d)


def mlp(x, mod, g2, g3, w1, w2, bt, tt):
    b, t, d = x.shape
    ff = w1.shape[1]
    tf = 1024
    return pl.pallas_call(
        _mlp_kernel,
        grid=(b // bt, t // tt, ff // tf),
        in_specs=[pl.BlockSpec((bt, tt, d), lambda i, j, f: (i, j, 0)),
                  pl.BlockSpec((bt, 1, d), lambda i, j, f: (i, 0, 3)),
                  pl.BlockSpec((bt, 1, d), lambda i, j, f: (i, 0, 4)),
                  pl.BlockSpec((bt, 1, d), lambda i, j, f: (i, 0, 5)),
                  pl.BlockSpec((1, d), lambda i, j, f: (0, 0)),
                  pl.BlockSpec((1, d), lambda i, j, f: (0, 0)),
                  pl.BlockSpec((d, tf), lambda i, j, f: (0, f)),
                  pl.BlockSpec((tf, d), lambda i, j, f: (f, 0))],
        out_specs=pl.BlockSpec((bt, tt, d), lambda i, j, f: (i, j, 0)),
        out_shape=jax.ShapeDtypeStruct((b, t, d), F32),
        scratch_shapes=[pltpu.VMEM((bt * tt, d), BF16), pltpu.VMEM((bt * tt, d), F32)],
        compiler_params=_cparams(("arbitrary", "arbitrary", "arbitrary")),
        name="mlp",
    )(x, mod, mod, mod, g2, g3, w1, w2)


def _combined_in_weight(w_in_l, w_gate_l):
    offs = np.concatenate([[0], np.cumsum(IN_SIZES)])
    (aq, ak, av, bglu, cq, ck, cv, iq, ik, iw, dqkv, dz, db, da) = [
        w_in_l[:, int(offs[i]):int(offs[i + 1])] for i in range(len(IN_SIZES))]
    d = w_in_l.shape[0]
    misc = jnp.concatenate([ik, iw, db, da, jnp.zeros((d, LANES - IDX_DIM - IDX_HEADS - 2 * D_HEADS), F32)], axis=1)
    pad = jnp.zeros((d, COL_GATE - COL_MISC - LANES), F32)
    gates = [w_gate_l[m] for m in range(N_BRANCH)]
    w = jnp.concatenate([aq, ak, av, bglu, cq, ck, cv, dqkv, dz, iq, misc, pad] + gates, axis=1)
    assert w.shape[1] == N_PROJ
    return w.astype(BF16)


def _lane_pad(v, offset):
    return jnp.zeros((1, LANES), F32).at[0, offset:offset + v.shape[0]].set(v)


def _run_group(x, mod, lw, past, bias, rel_bias, tiles):
    b, t, d = x.shape
    bt, tt, conv_tt = tiles
    proj = inproj(x, mod, lw['g'][0:1], lw['w_all'], bt, tt)

    if past is None:
        b_init = jnp.zeros((b, CONVB_HEAD, B_CH), F32)
        d_init = jnp.zeros((b, GDN_HEAD, D_CONV_CH), F32)
        s0 = jnp.zeros((b, D_HEADS, D_KDIM, D_VDIM), F32)
        o_a = flash_causal(proj, bias, rel_bias, lw['lam_init'], diff=True, qcol=COL_AQ, kcol=COL_AK, vcol=COL_AV,
                           bias_col0=0, a_lambda=lw['a_lambda'], a_norm_g=lw['a_norm_g'])
        sel = select_causal(proj)
        o_c = flash_causal(proj, bias, rel_bias, lw['lam_init'], diff=False, qcol=COL_CQ, kcol=COL_CK, vcol=COL_CV,
                           bias_col0=A_HEADS, mask=sel)
    else:
        layer = past['layer']
        lead_b = CONVB_HEAD - (B_WIDTH - 1)
        b_init = jnp.pad(past['b_conv'][layer], ((0, 0), (lead_b, 0), (0, 0)))
        d_init = jnp.pad(past['d_conv'][layer], ((0, 0), (GDN_HEAD - (D_CONV - 1), 0), (0, 0)))
        s0 = past['d_state'][layer]
        o_a = flash_full(proj, past['a_k'], past['a_v'], layer, bias, lw['lam_init'], diff=True,
                         qcol=COL_AQ, kcol=COL_AK, vcol=COL_AV, bias_col0=0,
                         a_lambda=lw['a_lambda'], a_norm_g=lw['a_norm_g'])
        sel = select_full(proj, past['c_kidx'], layer)
        o_c = flash_full(proj, past['c_k'], past['c_v'], layer, bias, lw['lam_init'], diff=False,
                         qcol=COL_CQ, kcol=COL_CK, vcol=COL_CV, bias_col0=A_HEADS, mask=sel)

    o_b, b_tail = conv_module(proj, b_init, lw['b_dw_w'], lw['b_dw_b'], lw['b_ln_g'], lw['b_ln_b'], conv_tt)
    o_d, d_tail, s_new = gated_delta(proj, d_init, s0, lw['d_conv_w'], lw['alog'], lw['dtb'], lw['d_norm_g'])

    mbt, mtt = (1, 256) if t >= 256 else (min(b, 256 // t), t)
    x1 = merge_out((o_a, o_b, o_c, o_d), proj, x, mod, lw['g'][1:2], lw['w_br'], lw['w_out'], mbt, mtt)
    fbt, ftt = (1, 512) if t >= 512 else (min(b, 512 // t), t)
    x2 = mlp(x1, mod, lw['g'][2:3], lw['g'][3:4], lw['w1'], lw['w2'], fbt, ftt)

    new = {
        'a_k': proj[:, :, COL_AK:COL_AK + A_QK].reshape(b, t, A_HEADS, 2 * A_DIM),
        'a_v': proj[:, :, COL_AV:COL_AV + A_QK].reshape(b, t, A_HEADS, 2 * A_DIM),
        'c_k': proj[:, :, COL_CK:COL_CK + C_W].reshape(b, t, C_HEADS, C_DIM),
        'c_v': proj[:, :, COL_CV:COL_CV + C_W].reshape(b, t, C_HEADS, C_DIM),
        'c_kidx': proj[:, :, COL_MISC:COL_MISC + IDX_DIM],
        'b_conv': b_tail[:, CONVB_HEAD - (B_WIDTH - 1):, :],
        'd_conv': d_tail[:, GDN_HEAD - (D_CONV - 1):, :],
        'd_state': s_new,
    }
    return x2, new


def kernel(x_prompt, x_sample, c_prompt, c_sample, cache_a_k, cache_a_v, cache_c_k, cache_c_v, cache_c_kidx, state_b_conv, state_d_conv, state_d_state, rel_bias, ada_w, ada_b, norm_g, w_in, a_lambda, a_norm_g, b_dw_w, b_dw_b, b_ln_g, b_ln_b, d_conv_w, d_a_log, d_dt_bias, d_norm_g, w_gate, w_br, w_out, mlp_w1, mlp_w2):
    depth = w_in.shape[0]
    bp, tp, d = x_prompt.shape
    bs, ts, _ = x_sample.shape
    past_len = cache_a_k.shape[2]

    mod_all = adaln_mod(jnp.concatenate([c_prompt, c_sample], axis=0), ada_w, ada_b)
    bias_p = bias_tiles(rel_bias, FLASH_T, 2 * FLASH_T, FLASH_T, 0)
    bias_s = bias_tiles(rel_bias, ts, past_len + ts, past_len, 0)

    past = {
        'a_k': cache_a_k.reshape(depth, bs, past_len, A_QK),
        'a_v': cache_a_v.reshape(depth, bs, past_len, A_QK),
        'c_k': cache_c_k.reshape(depth, bs, past_len, C_W),
        'c_v': cache_c_v.reshape(depth, bs, past_len, C_W),
        'c_kidx': cache_c_kidx, 'b_conv': state_b_conv, 'd_conv': state_d_conv, 'd_state': state_d_state,
    }
    names = ('a_k', 'a_v', 'c_k', 'c_v', 'c_kidx', 'b_conv', 'd_conv', 'd_state')
    new_p = {n: [] for n in names}
    new_s = {n: [] for n in names}
    xp, xs = x_prompt, x_sample
    for l in range(depth):
        lw = {
            'g': norm_g[l],
            'w_all': _combined_in_weight(w_in[l], w_gate[l]),
            'lam_init': jnp.full((1,), 0.8 - 0.6 * math.exp(-0.3 * l), F32),
            'a_lambda': a_lambda[l],
            'a_norm_g': a_norm_g[l].reshape(1, LANES),
            'b_dw_w': b_dw_w[l], 'b_dw_b': b_dw_b[l].reshape(1, B_CH),
            'b_ln_g': b_ln_g[l].reshape(1, B_CH), 'b_ln_b': b_ln_b[l].reshape(1, B_CH),
            'd_conv_w': d_conv_w[l],
            'alog': _lane_pad(d_a_log[l], MISC_DA), 'dtb': _lane_pad(d_dt_bias[l], MISC_DA),
            'd_norm_g': d_norm_g[l].reshape(1, LANES),
            'w_br': w_br[l].astype(BF16), 'w_out': w_out[l].astype(BF16),
            'w1': mlp_w1[l].astype(BF16), 'w2': mlp_w2[l].astype(BF16),
        }
        mod_p = mod_all[l, :bp].reshape(bp, 1, 6 * d)
        mod_s = mod_all[l, bp:].reshape(bs, 1, 6 * d)
        xp, sp = _run_group(xp, mod_p, lw, None, bias_p, rel_bias, (1, 1024, 256))
        xs, ss = _run_group(xs, mod_s, lw, dict(past, layer=l), bias_s, rel_bias, (min(bs, 1024 // ts), ts, ts))
        for n in names:
            new_p[n].append(sp[n])
            new_s[n].append(ss[n])
    return (xp, xs) + tuple(jnp.stack(new_p[n]) for n in names) + tuple(jnp.stack(new_s[n]) for n in names)
```

```python
import functools
import math

import numpy as np
import jax
import jax.numpy as jnp
from jax import lax
from jax.experimental import pallas as pl
from jax.experimental.pallas import tpu as pltpu

F32 = jnp.float32
BF16 = jnp.bfloat16
HIGHEST = lax.Precision.HIGHEST

D_MODEL = 1024
CHUNK = 64
BRANCH_W = D_MODEL // 2
N_BRANCH = 4
A_HEADS = 4
A_DIM = 64
B_CH = BRANCH_W
B_WIDTH = 31
C_HEADS = 8
C_DIM = 64
IDX_HEADS = 4
IDX_DIM = 64
TOPK_MAX = 256
D_HEADS = 4
D_VDIM = 128
D_KDIM = 64
D_CONV = 4
NUM_BUCKETS = 32
MAX_DISTANCE = 128
D_FF = 4 * D_MODEL
EPS = 1e-6
A_QK = A_HEADS * 2 * A_DIM
C_W = C_HEADS * C_DIM
D_QK = D_HEADS * D_KDIM
D_V = D_HEADS * D_VDIM
D_CONV_CH = 2 * D_QK + D_V
IN_SIZES = (A_QK, A_QK, A_QK, 2 * B_CH, C_W, C_W, C_W, IDX_HEADS * IDX_DIM, IDX_DIM, IDX_HEADS,
            D_CONV_CH, D_V, D_HEADS, D_HEADS)

LANES = 128
NEG = -1e30
LOG2E = 1.4426950408889634
INT_MIN = -2 ** 31
VMEM_LIMIT = 56 * 1024 * 1024

COL_AQ, COL_AK, COL_AV = 0, 512, 1024
COL_BGLU = 1536
COL_CQ, COL_CK, COL_CV = 2560, 3072, 3584
COL_DQKV = 4096
COL_DZ = 5120
COL_IQ = 5632
COL_MISC = 5888
COL_GATE = 6144
N_PROJ = 10240
MISC_IW, MISC_DB, MISC_DA = 64, 68, 72

FLASH_T = 256
FLASH_FAR = 1024
SEL_TQ = 128
SEL_KB = 256
SEL_ROWS = 512
SEL_PRE = 256
I16_MIN = -2 ** 15


def _cparams(sem):
    return pltpu.CompilerParams(dimension_semantics=sem, vmem_limit_bytes=VMEM_LIMIT)


def _dot(a, b, precision=None):
    return jnp.dot(a, b, preferred_element_type=F32, precision=precision)


def _dot_nt(a, b):
    return lax.dot_general(a, b, (((1,), (1,)), ((), ())), preferred_element_type=F32)


def _dot_tn(a, b):
    return lax.dot_general(a, b, (((0,), (0,)), ((), ())), preferred_element_type=F32)


def _rms(x, g):
    return x * lax.rsqrt(jnp.mean(x * x, axis=-1, keepdims=True) + EPS) * g


def _silu(x):
    return x * jax.nn.sigmoid(x)


def _chunk_of(pos):
    return jnp.right_shift(pos, int(math.log2(CHUNK)))


def _mod_kernel(c_ref, w_ref, b_ref, o_ref):
    s = _silu(c_ref[...])
    o_ref[0] = _dot(s.astype(BF16), w_ref[0].astype(BF16)) + b_ref[0]


def adaln_mod(c_all, ada_w, ada_b):
    depth, d, n = ada_w.shape
    bc = c_all.shape[0]
    tn = 1024
    return pl.pallas_call(
        _mod_kernel,
        grid=(depth, n // tn),
        in_specs=[pl.BlockSpec((bc, d), lambda l, j: (0, 0)),
                  pl.BlockSpec((1, d, tn), lambda l, j: (l, 0, j)),
                  pl.BlockSpec((1, 1, tn), lambda l, j: (l, 0, j))],
        out_specs=pl.BlockSpec((1, bc, tn), lambda l, j: (l, 0, j)),
        out_shape=jax.ShapeDtypeStruct((depth, bc, n), F32),
        compiler_params=_cparams(("arbitrary", "arbitrary")),
        name="adaln_mod",
    )(c_all, ada_w, ada_b.reshape(depth, 1, n))


def _bias_kernel(tab_ref, o_ref, *, q0, k0):
    h = pl.program_id(0)
    tq, w = o_ref.shape[1], o_ref.shape[2]
    row = lax.broadcasted_iota(jnp.int32, (tq, w), 0)
    col = lax.broadcasted_iota(jnp.int32, (tq, w), 1)
    rel = (col + k0) - (row + q0)
    nb = NUM_BUCKETS // 2
    max_exact = nb // 2
    n = jnp.abs(rel)
    large = max_exact + (jnp.log(jnp.maximum(n, max_exact).astype(F32) / max_exact)
                         / math.log(MAX_DISTANCE / max_exact) * (nb - max_exact)).astype(jnp.int32)
    large = jnp.minimum(large, nb - 1)
    bucket = jnp.where(rel > 0, nb, 0) + jnp.where(n < max_exact, n, large)
    val = jnp.zeros((tq, w), F32)
    for bk in range(NUM_BUCKETS):
        val = jnp.where(bucket == bk, tab_ref[bk, h], val)
    o_ref[0] = val * LOG2E


def bias_tiles(rel_bias, tq, w, q0, k0):
    nh = rel_bias.shape[1]
    return pl.pallas_call(
        functools.partial(_bias_kernel, q0=q0, k0=k0),
        grid=(nh,),
        in_specs=[pl.BlockSpec(memory_space=pltpu.SMEM)],
        out_specs=pl.BlockSpec((1, tq, w), lambda h: (h, 0, 0)),
        out_shape=jax.ShapeDtypeStruct((nh, tq, w), F32),
        compiler_params=_cparams(("arbitrary",)),
        name="bias_tiles",
    )(rel_bias)


def _inproj_kernel(x_ref, sh_ref, sc_ref, g_ref, w_ref, o_ref, h_ref):
    bt, tt, d = x_ref.shape

    @pl.when(pl.program_id(2) == 0)
    def _():
        h = _rms(x_ref[...], g_ref[...]) * (1.0 + sc_ref[...]) + sh_ref[...]
        h_ref[...] = h.reshape(bt * tt, d).astype(BF16)

    o_ref[...] = _dot(h_ref[...], w_ref[...]).reshape(o_ref.shape)


def inproj(x, mod, g, w_all, bt, tt):
    b, t, d = x.shape
    n = w_all.shape[1]
    tn = 1024
    return pl.pallas_call(
        _inproj_kernel,
        grid=(b // bt, t // tt, n // tn),
        in_specs=[pl.BlockSpec((bt, tt, d), lambda i, j, k: (i, j, 0)),
                  pl.BlockSpec((bt, 1, d), lambda i, j, k: (i, 0, 0)),
                  pl.BlockSpec((bt, 1, d), lambda i, j, k: (i, 0, 1)),
                  pl.BlockSpec((1, d), lambda i, j, k: (0, 0)),
                  pl.BlockSpec((d, tn), lambda i, j, k: (0, k))],
        out_specs=pl.BlockSpec((bt, tt, tn), lambda i, j, k: (i, j, k)),
        out_shape=jax.ShapeDtypeStruct((b, t, n), F32),
        scratch_shapes=[pltpu.VMEM((bt * tt, d), BF16)],
        compiler_params=_cparams(("arbitrary", "arbitrary", "arbitrary")),
        name="inproj",
    )(x, mod, mod, g, w_all)


def _flash_tile(qs, kb, vb, biases, shifts, add_mask, keep, m_ref, l_ref, acc_ref):
    scores = [_dot_nt(q, kb) for q in qs]
    for m, s in enumerate(scores):
        if biases[m] is not None:
            s = s + biases[m]
        if add_mask is not None:
            s = s + add_mask
        if keep is not None:
            s = jnp.where(keep, s, NEG)
        m_prev = m_ref[m]
        s_max = jnp.max(s, axis=-1, keepdims=True)
        if shifts[m] is not None:
            s_max = s_max + shifts[m]
        m_new = jnp.maximum(m_prev, s_max)
        alpha = jnp.exp2(m_prev - m_new)
        p = jnp.exp2(s - (m_new if shifts[m] is None else m_new - shifts[m]))
        l_ref[m] = alpha * l_ref[m] + jnp.sum(p, axis=-1, keepdims=True)
        acc_ref[m] = alpha * acc_ref[m] + _dot(p.astype(BF16), vb)
        m_ref[m] = m_new


def _flash_finish(diff, lam_ref, alam_ref, g_ref, o_ref, l_ref, acc_ref, lane):
    o0 = acc_ref[0] / l_ref[0]
    o1 = acc_ref[1] / l_ref[1]
    if diff:
        lv = alam_ref[...]
        lam_init = lam_ref[0]
        lam = (jnp.exp(jnp.sum(lv[0:1] * lv[1:2], axis=-1, keepdims=True))
               - jnp.exp(jnp.sum(lv[2:3] * lv[3:4], axis=-1, keepdims=True)) + lam_init)
        o = o0 - lam * o1
        o = _rms(o, g_ref[...]) * (1.0 - lam_init)
    else:
        o = jnp.where(lane < C_DIM, o0, o1)
    o_ref[0] = o


def _flash_init(q_ref, m_ref, l_ref, acc_ref):
    tq = q_ref.shape[1]
    lane = lax.broadcasted_iota(jnp.int32, (1, LANES), 1)
    q = q_ref[0] * (A_DIM ** -0.5 * LOG2E)
    q0 = jnp.where(lane < A_DIM, q, 0.0).astype(BF16)
    q1 = jnp.where(lane >= A_DIM, q, 0.0).astype(BF16)
    m_ref[...] = jnp.full(m_ref.shape, NEG, F32)
    l_ref[...] = jnp.zeros(l_ref.shape, F32)
    acc_ref[...] = jnp.zeros(acc_ref.shape, F32)
    return lane, (q0, q1)


def _flash_causal_kernel(*refs, diff, has_mask, col0):
    tab_ref, lam_ref, q_ref, k_ref, v_ref, bias_ref = refs[:6]
    pos = 6
    mask_ref = None
    if has_mask:
        mask_ref = refs[pos]
        pos += 1
    alam_ref = g_ref = None
    if diff:
        alam_ref, g_ref = refs[pos], refs[pos + 1]
        pos += 2
    o_ref, m_ref, l_ref, acc_ref = refs[pos:pos + 4]
    t = q_ref.shape[1]
    nb = bias_ref.shape[0]
    p = pl.program_id(1)
    i = pl.program_id(2)
    lane, qs = _flash_init(q_ref, m_ref, l_ref, acc_ref)
    far = [tab_ref[NUM_BUCKETS // 2 - 1, col0 + nb * p + min(m, nb - 1)] * LOG2E for m in range(2)]

    def tile(ks, width, biases, shifts, keep):
        kb = k_ref[0, pl.ds(ks, width), :].astype(BF16)
        vb = v_ref[0, pl.ds(ks, width), :].astype(BF16)
        add = mask_ref[0, :, pl.ds(ks, width)].astype(F32) if has_mask else None
        _flash_tile(qs, kb, vb, biases, shifts, add, keep, m_ref, l_ref, acc_ref)

    n_far = jnp.maximum(i - 1, 0)
    wide = FLASH_FAR // t
    n_wide = n_far // wide

    def wide_body(kj, carry):
        tile(pl.multiple_of(kj * FLASH_FAR, FLASH_FAR), FLASH_FAR, (None, None), far, None)
        return carry

    lax.fori_loop(0, n_wide, wide_body, 0)
    width = FLASH_FAR // 2
    done = n_wide * wide
    while width >= t:
        has = ((n_far - done) * t) >= width
        start = pl.multiple_of(done * t, t)

        @pl.when(has)
        def _(start=start, width=width):
            tile(start, width, (None, None), far, None)

        done = done + jnp.where(has, width // t, 0)
        width //= 2

    row = lax.broadcasted_iota(jnp.int32, (t, 2 * t), 0)
    col = lax.broadcasted_iota(jnp.int32, (t, 2 * t), 1)
    keep = _chunk_of(col) <= _chunk_of(row) + t // CHUNK

    @pl.when(i == 0)
    def _():
        tile(0, t, [bias_ref[min(m, nb - 1), :, t:2 * t] for m in range(2)], (None, None), keep[:, t:2 * t])

    @pl.when(i >= 1)
    def _():
        tile(pl.multiple_of((i - 1) * t, t), 2 * t, [bias_ref[min(m, nb - 1)] for m in range(2)], (None, None), keep)

    _flash_finish(diff, lam_ref, alam_ref, g_ref, o_ref, l_ref, acc_ref, lane)


def flash_causal(proj, near_bias, rel_bias, lam_init, *, diff, qcol, kcol, vcol, bias_col0,
                 mask=None, a_lambda=None, a_norm_g=None):
    b, t_all, _ = proj.shape
    t = FLASH_T
    npair = 4
    nb = 1 if diff else 2
    qb, kb, vb = qcol // LANES, kcol // LANES, vcol // LANES
    bb0 = bias_col0 // nb
    in_specs = [pl.BlockSpec(memory_space=pltpu.SMEM),
                pl.BlockSpec(memory_space=pltpu.SMEM),
                pl.BlockSpec((1, t, LANES), lambda bi, p, i: (bi, i, qb + p)),
                pl.BlockSpec((1, t_all, LANES), lambda bi, p, i: (bi, 0, kb + p)),
                pl.BlockSpec((1, t_all, LANES), lambda bi, p, i: (bi, 0, vb + p)),
                pl.BlockSpec((nb, t, 2 * t), lambda bi, p, i: (bb0 + p, 0, 0))]
    args = [rel_bias, lam_init, proj, proj, proj, near_bias]
    if mask is not None:
        in_specs.append(pl.BlockSpec((1, t, t_all), lambda bi, p, i: (bi, i, 0)))
        args.append(mask)
    if diff:
        in_specs += [pl.BlockSpec((4, A_DIM), lambda bi, p, i: (0, 0)),
                     pl.BlockSpec((1, LANES), lambda bi, p, i: (0, 0))]
        args += [a_lambda, a_norm_g]
    return pl.pallas_call(
        functools.partial(_flash_causal_kernel, diff=diff, has_mask=mask is not None, col0=bias_col0),
        grid=(b, npair, t_all // t),
        in_specs=in_specs,
        out_specs=pl.BlockSpec((1, t, LANES), lambda bi, p, i: (bi, i, p)),
        out_shape=jax.ShapeDtypeStruct((b, t_all, npair * LANES), F32),
        scratch_shapes=[pltpu.VMEM((2, t, 1), F32), pltpu.VMEM((2, t, 1), F32), pltpu.VMEM((2, t, LANES), F32)],
        compiler_params=_cparams(("arbitrary", "arbitrary", "arbitrary")),
        name="flash_diff" if diff else "flash_sel",
    )(*args)


def _flash_full_kernel(*refs, diff, has_mask, tk):
    lam_ref, q_ref, kc_ref, vc_ref, kn_ref, vn_ref, bias_ref = refs[:7]
    pos = 7
    mask_ref = None
    if has_mask:
        mask_ref = refs[pos]
        pos += 1
    alam_ref = g_ref = None
    if diff:
        alam_ref, g_ref = refs[pos], refs[pos + 1]
        pos += 2
    o_ref, m_ref, l_ref, acc_ref = refs[pos:pos + 4]
    past = kc_ref.shape[2]
    tn = kn_ref.shape[1]
    nb = bias_ref.shape[0]
    lane, qs = _flash_init(q_ref, m_ref, l_ref, acc_ref)

    def tile(kb, vb, c0, width):
        add = mask_ref[0, :, c0:c0 + width].astype(F32) if has_mask else None
        _flash_tile(qs, kb, vb, [bias_ref[min(m, nb - 1), :, c0:c0 + width] for m in range(2)], (None, None),
                    add, None, m_ref, l_ref, acc_ref)

    for j in range(past // tk):
        tile(kc_ref[0, 0, j * tk:(j + 1) * tk, :].astype(BF16), vc_ref[0, 0, j * tk:(j + 1) * tk, :].astype(BF16),
             j * tk, tk)
    tile(kn_ref[0].astype(BF16), vn_ref[0].astype(BF16), past, tn)
    _flash_finish(diff, lam_ref, alam_ref, g_ref, o_ref, l_ref, acc_ref, lane)


def flash_full(proj, k_cache, v_cache, layer, full_bias, lam_init, *, diff, qcol, kcol, vcol, bias_col0,
               mask=None, a_lambda=None, a_norm_g=None):
    b, tq, _ = proj.shape
    past = k_cache.shape[2]
    npair = 4
    nb = 1 if diff else 2
    ltot = past + tq
    qb, kb, vb = qcol // LANES, kcol // LANES, vcol // LANES
    bb0 = bias_col0 // nb
    in_specs = [pl.BlockSpec(memory_space=pltpu.SMEM),
                pl.BlockSpec((1, tq, LANES), lambda bi, p: (bi, 0, qb + p)),
                pl.BlockSpec((1, 1, past, LANES), lambda bi, p: (layer, bi, 0, p)),
                pl.BlockSpec((1, 1, past, LANES), lambda bi, p: (layer, bi, 0, p)),
                pl.BlockSpec((1, tq, LANES), lambda bi, p: (bi, 0, kb + p)),
                pl.BlockSpec((1, tq, LANES), lambda bi, p: (bi, 0, vb + p)),
                pl.BlockSpec((nb, tq, ltot), lambda bi, p: (bb0 + p, 0, 0))]
    args = [lam_init, proj, k_cache, v_cache, proj, proj, full_bias]
    if mask is not None:
        in_specs.append(pl.BlockSpec((1, tq, mask.shape[2]), lambda bi, p: (bi, 0, 0)))
        args.append(mask)
    if diff:
        in_specs += [pl.BlockSpec((4, A_DIM), lambda bi, p: (0, 0)),
                     pl.BlockSpec((1, LANES), lambda bi, p: (0, 0))]
        args += [a_lambda, a_norm_g]
    return pl.pallas_call(
        functools.partial(_flash_full_kernel, diff=diff, has_mask=mask is not None, tk=256),
        grid=(b, npair),
        in_specs=in_specs,
        out_specs=pl.BlockSpec((1, tq, LANES), lambda bi, p: (bi, 0, p)),
        out_shape=jax.ShapeDtypeStruct((b, tq, npair * LANES), F32),
        scratch_shapes=[pltpu.VMEM((2, tq, 1), F32), pltpu.VMEM((2, tq, 1), F32), pltpu.VMEM((2, tq, LANES), F32)],
        compiler_params=_cparams(("arbitrary", "arbitrary")),
        name="flash_diff_full" if diff else "flash_sel_full",
    )(*args)


def _index_operands(qi, misc, width):
    tq = qi.shape[0]
    w = misc[:, MISC_IW:MISC_IW + IDX_HEADS] * (IDX_HEADS ** -0.5 * IDX_DIM ** -0.5)
    qh = [qi[:, h * IDX_DIM:(h + 1) * IDX_DIM].astype(BF16) for h in range(IDX_HEADS)]
    wh = [jnp.broadcast_to(w[:, h:h + 1], (tq, width)) for h in range(IDX_HEADS)]
    return qh, wh


def _index_scores(qh, wh, kb):
    isc = None
    for q, w in zip(qh, wh):
        term = w[:, :kb.shape[0]] * jnp.maximum(_dot_nt(q, kb), 0.0)
        isc = term if isc is None else isc + term
    return isc + 0.0


def _sortable(x):
    bits = lax.bitcast_convert_type(x, jnp.int32)
    return jnp.where(bits < 0, bits ^ jnp.int32(0x7FFFFFFF), bits)


def _store_keys(keys_ref, hi_ref, cs, key):
    width = key.shape[1]
    keys_ref[:, pl.ds(cs, width)] = key
    hi_ref[:, pl.ds(cs, width)] = jnp.right_shift(key, 16).astype(jnp.int16)


def _select_topk(keys_ref, hi_ref, lo_ref, nkb, k_sel, out_ref):
    tq = keys_ref.shape[0]

    def block(ref, j):
        return ref[:, pl.ds(pl.multiple_of(j * SEL_KB, SEL_KB), SEL_KB)]

    def count_ge(ref, cand):
        cb = jnp.broadcast_to(cand, (tq, SEL_KB)).astype(jnp.int16)

        def body(j, acc):
            return acc + jnp.where(block(ref, j) >= cb, jnp.int16(1), jnp.int16(0))

        acc = lax.fori_loop(0, nkb, body, jnp.zeros((tq, SEL_KB), jnp.int16))
        return jnp.sum(acc.astype(jnp.int32), axis=-1, keepdims=True)

    def count_gt(ref, t):
        top = -I16_MIN - 1
        return jnp.where(t >= top, 0, count_ge(ref, jnp.minimum(t + 1, top)))

    def search(ref, base):
        zero = jnp.zeros((tq, 1), jnp.int32)
        t0 = jnp.where(base + count_ge(ref, zero) >= k_sel, zero, jnp.full((tq, 1), I16_MIN, jnp.int32))

        def bit_body(it, t):
            cand = t + jnp.left_shift(jnp.int32(1), 14 - it)
            return jnp.where(base + count_ge(ref, cand) >= k_sel, cand, t)

        return lax.fori_loop(0, 15, bit_body, t0)

    hi = search(hi_ref, jnp.zeros((tq, 1), jnp.int32))
    above = count_gt(hi_ref, hi)
    hib = jnp.broadcast_to(hi, (tq, SEL_KB))

    def lo_body(j, carry):
        key = block(keys_ref, j)
        lo = jnp.bitwise_and(key, 0xFFFF) + I16_MIN
        lo = jnp.where(jnp.right_shift(key, 16) == hib, lo, I16_MIN)
        lo_ref[:, pl.ds(pl.multiple_of(j * SEL_KB, SEL_KB), SEL_KB)] = lo.astype(jnp.int16)
        return carry

    lax.fori_loop(0, nkb, lo_body, 0)
    lo = search(lo_ref, above)
    thr = hi * 65536 + (lo - I16_MIN)
    need = k_sel - (above + count_gt(lo_ref, lo))
    need = jnp.where(thr == INT_MIN, 0, need).astype(F32)
    thrb = jnp.broadcast_to(thr, (tq, SEL_KB))
    needb = jnp.broadcast_to(need, (tq, SEL_KB))
    r = lax.broadcasted_iota(jnp.int32, (SEL_KB, 2 * SEL_KB), 0)
    c = lax.broadcasted_iota(jnp.int32, (SEL_KB, 2 * SEL_KB), 1)
    tri = jnp.where((r <= c) | (c >= SEL_KB), 1.0, 0.0).astype(BF16)

    def mask_body(j, offs):
        blk = block(keys_ref, j)
        eq = blk == thrb
        cnt = _dot(jnp.where(eq, 1.0, 0.0).astype(BF16), tri)
        sel = (blk > thrb) | (eq & (offs + cnt[:, :SEL_KB] <= needb))
        out_ref[0, :, pl.ds(pl.multiple_of(j * SEL_KB, SEL_KB), SEL_KB)] = jnp.where(sel, 0.0, NEG).astype(out_ref.dtype)
        return offs + cnt[:, SEL_KB:]

    lax.fori_loop(0, nkb, mask_body, jnp.zeros((tq, SEL_KB), F32))


def _tree_sum(parts):
    while len(parts) > 1:
        parts = [parts[k] + parts[k + 1] for k in range(0, len(parts) - 1, 2)] + parts[len(parts) & ~1:]
    return parts[0]


def _select_causal_kernel(iq_ref, mq_ref, mk_ref, o_ref, keys_ref, hi_ref, lo_ref, *, k_sel):
    tq = iq_ref.shape[1]
    i = pl.program_id(1)
    nkb = (i * tq) // SEL_ROWS + 1
    sub = 16
    qt = jnp.transpose(iq_ref[0])
    mt = jnp.transpose(mq_ref[0])
    qh = [qt[h * IDX_DIM:(h + 1) * IDX_DIM, :].astype(BF16) for h in range(IDX_HEADS)]
    wh = [mt[MISC_IW + h:MISC_IW + h + 1, :] * (IDX_HEADS ** -0.5 * IDX_DIM ** -0.5) for h in range(IDX_HEADS)]
    qpos = lax.broadcasted_iota(jnp.int32, (SEL_ROWS, tq), 1) + i * tq
    kpos = lax.broadcasted_iota(jnp.int32, (SEL_ROWS, tq), 0)

    def rows(j):
        return pl.ds(pl.multiple_of(j * SEL_ROWS, SEL_ROWS), SEL_ROWS)

    def score_body(j, carry):
        kb = mk_ref[0, rows(j), :][:, 0:IDX_DIM].astype(BF16)
        isc = _tree_sum([w * jnp.maximum(_dot(kb, q), 0.0) for q, w in zip(qh, wh)]) + 0.0
        vis = _chunk_of(kpos + j * SEL_ROWS) <= _chunk_of(qpos)
        key = jnp.where(vis, _sortable(isc), INT_MIN)
        keys_ref[rows(j), :] = key
        hi_ref[rows(j), :] = jnp.right_shift(key, 16).astype(jnp.int16)
        return carry

    lax.fori_loop(0, nkb, score_body, 0)

    def count_ge(ref, cand):
        cb = jnp.broadcast_to(cand, (sub, tq)).astype(jnp.int16)

        def body(j, acc):
            blk = ref[rows(j), :]
            hits = [jnp.where(blk[k * sub:(k + 1) * sub] >= cb, jnp.int16(1), jnp.int16(0))
                    for k in range(SEL_ROWS // sub)]
            return acc + _tree_sum(hits)

        acc = lax.fori_loop(0, nkb, body, jnp.zeros((sub, tq), jnp.int16))
        return jnp.sum(acc.astype(jnp.int32), axis=0, keepdims=True)

    def count_gt(ref, t):
        top = -I16_MIN - 1
        return jnp.where(t >= top, 0, count_ge(ref, jnp.minimum(t + 1, top)))

    def search(ref, base):
        zero = jnp.zeros((1, tq), jnp.int32)
        t0 = jnp.where(base + count_ge(ref, zero) >= k_sel, zero, jnp.full((1, tq), I16_MIN, jnp.int32))

        def bit_body(it, t):
            cand = t + jnp.left_shift(jnp.int32(1), 14 - it)
            return jnp.where(base + count_ge(ref, cand) >= k_sel, cand, t)

        return lax.fori_loop(0, 15, bit_body, t0)

    hi = search(hi_ref, jnp.zeros((1, tq), jnp.int32))
    above = count_gt(hi_ref, hi)

    def lo_body(j, carry):
        key = keys_ref[rows(j), :]
        lo = jnp.bitwise_and(key, 0xFFFF) + I16_MIN
        lo = jnp.where(jnp.right_shift(key, 16) == hi, lo, I16_MIN)
        lo_ref[rows(j), :] = lo.astype(jnp.int16)
        return carry

    lax.fori_loop(0, nkb, lo_body, 0)
    lo = search(lo_ref, above)
    thr = hi * 65536 + (lo - I16_MIN)
    need = k_sel - (above + count_gt(lo_ref, lo))
    need = jnp.where(thr == INT_MIN, 0, need).astype(F32)
    r = lax.broadcasted_iota(jnp.int32, (2 * SEL_PRE, SEL_PRE), 0)
    c = lax.broadcasted_iota(jnp.int32, (2 * SEL_PRE, SEL_PRE), 1)
    tri = jnp.where((c <= r) | (r >= SEL_PRE), 1.0, 0.0).astype(BF16)
    r = lax.broadcasted_iota(jnp.int32, (tq, tq), 0)
    c = lax.broadcasted_iota(jnp.int32, (tq, tq), 1)
    eye = jnp.where(r == c, 1.0, 0.0).astype(BF16)

    def mask_body(j, offs):
        blk = keys_ref[rows(j), :]
        parts = [blk[k * SEL_PRE:(k + 1) * SEL_PRE] for k in range(SEL_ROWS // SEL_PRE)]
        eqs = [p == thr for p in parts]
        cnts = [_dot(tri, jnp.where(eq, 1.0, 0.0).astype(BF16)) for eq in eqs]
        sels = []
        for p, eq, cnt in zip(parts, eqs, cnts):
            sel = (p > thr) | (eq & (offs + cnt[:SEL_PRE] <= need))
            sels.append(jnp.where(sel, 1.0, 0.0).astype(BF16))
            offs = offs + cnt[SEL_PRE:SEL_PRE + 1]
        picked = _dot_nt(eye, jnp.concatenate(sels, axis=0))
        o_ref[0, :, rows(j)] = ((picked - 1.0) * -NEG).astype(o_ref.dtype)
        return offs

    lax.fori_loop(0, nkb, mask_body, jnp.zeros((1, tq), F32))


def select_causal(proj):
    b, t, _ = proj.shape
    tq = SEL_TQ
    k_sel = min(TOPK_MAX, t // 4)
    return pl.pallas_call(
        functools.partial(_select_causal_kernel, k_sel=k_sel),
        grid=(b, t // tq),
        in_specs=[pl.BlockSpec((1, tq, IDX_HEADS * IDX_DIM), lambda bi, i: (bi, i, COL_IQ // (IDX_HEADS * IDX_DIM))),
                  pl.BlockSpec((1, tq, LANES), lambda bi, i: (bi, i, COL_MISC // LANES)),
                  pl.BlockSpec((1, t, LANES), lambda bi, i: (bi, 0, COL_MISC // LANES))],
        out_specs=pl.BlockSpec((1, tq, t), lambda bi, i: (bi, i, 0)),
        out_shape=jax.ShapeDtypeStruct((b, t, t), BF16),
        scratch_shapes=[pltpu.VMEM((t, tq), jnp.int32), pltpu.VMEM((t, tq), jnp.int16),
                        pltpu.VMEM((t, tq), jnp.int16)],
        compiler_params=_cparams(("arbitrary", "arbitrary")),
        name="select_causal",
    )(proj, proj, proj)


def _select_full_kernel(iq_ref, mq_ref, kc_ref, o_ref, keys_ref, hi_ref, lo_ref, *, k_sel):
    tq = iq_ref.shape[1]
    past = kc_ref.shape[2]
    nkb = keys_ref.shape[1] // SEL_KB
    misc = mq_ref[0]
    qh, wh = _index_operands(iq_ref[0], misc, SEL_KB)
    for j in range(past // SEL_KB):
        kb = kc_ref[0, 0, j * SEL_KB:(j + 1) * SEL_KB, :].astype(BF16)
        _store_keys(keys_ref, hi_ref, j * SEL_KB, _sortable(_index_scores(qh, wh, kb)))
    _store_keys(keys_ref, hi_ref, past, jnp.full((tq, keys_ref.shape[1] - past), INT_MIN, jnp.int32))
    _store_keys(keys_ref, hi_ref, past, _sortable(_index_scores(qh, wh, misc[:, 0:IDX_DIM].astype(BF16))))
    _select_topk(keys_ref, hi_ref, lo_ref, nkb, k_sel, o_ref)


def select_full(proj, kidx_cache, layer):
    b, tq, _ = proj.shape
    past = kidx_cache.shape[2]
    ltot = past + tq
    lpad = -(-ltot // SEL_KB) * SEL_KB
    k_sel = min(TOPK_MAX, ltot // 4)
    return pl.pallas_call(
        functools.partial(_select_full_kernel, k_sel=k_sel),
        grid=(b,),
        in_specs=[pl.BlockSpec((1, tq, IDX_HEADS * IDX_DIM), lambda bi: (bi, 0, COL_IQ // (IDX_HEADS * IDX_DIM))),
                  pl.BlockSpec((1, tq, LANES), lambda bi: (bi, 0, COL_MISC // LANES)),
                  pl.BlockSpec((1, 1, past, IDX_DIM), lambda bi: (layer, bi, 0, 0))],
        out_specs=pl.BlockSpec((1, tq, lpad), lambda bi: (bi, 0, 0)),
        out_shape=jax.ShapeDtypeStruct((b, tq, lpad), BF16),
        scratch_shapes=[pltpu.VMEM((tq, lpad), jnp.int32), pltpu.VMEM((tq, lpad), jnp.int16),
                        pltpu.VMEM((tq, lpad), jnp.int16)],
        compiler_params=_cparams(("arbitrary",)),
        name="select_full",
    )(proj, proj, kidx_cache)


CONVB_ROWS = 64
CONVB_HEAD = 32


def _convb_kernel(a_ref, gt_ref, init_ref, w_ref, dwb_ref, lng_ref, lnb_ref, o_ref, tail_ref, f_ref):
    tt = a_ref.shape[1]
    lead = CONVB_HEAD - (B_WIDTH - 1)

    @pl.when(pl.program_id(1) == 0)
    def _():
        f_ref[0:CONVB_HEAD, :] = init_ref[0]

    f_ref[CONVB_HEAD:CONVB_HEAD + tt, :] = a_ref[0] * jax.nn.sigmoid(gt_ref[0])
    for r in range(tt // CONVB_ROWS):
        acc = None
        for k in range(B_WIDTH):
            s0 = r * CONVB_ROWS + lead + k
            term = w_ref[k:k + 1, :] * f_ref[s0:s0 + CONVB_ROWS, :]
            acc = term if acc is None else acc + term
        y = acc + dwb_ref[...]
        yc = y - jnp.mean(y, axis=-1, keepdims=True)
        yn = yc * lax.rsqrt(jnp.mean(yc * yc, axis=-1, keepdims=True) + EPS) * lng_ref[...] + lnb_ref[...]
        o_ref[0, r * CONVB_ROWS:(r + 1) * CONVB_ROWS, :] = _silu(yn)
    last = f_ref[tt:tt + CONVB_HEAD, :]
    tail_ref[0] = last
    f_ref[0:CONVB_HEAD, :] = last


def conv_module(proj, init, w, dwb, lng, lnb, tt):
    b, t, _ = proj.shape
    cb = COL_BGLU // B_CH
    vec = pl.BlockSpec((1, B_CH), lambda bi, j: (0, 0))
    return pl.pallas_call(
        _convb_kernel,
        grid=(b, t // tt),
        in_specs=[pl.BlockSpec((1, tt, B_CH), lambda bi, j: (bi, j, cb)),
                  pl.BlockSpec((1, tt, B_CH), lambda bi, j: (bi, j, cb + 1)),
                  pl.BlockSpec((1, CONVB_HEAD, B_CH), lambda bi, j: (bi, 0, 0)),
                  pl.BlockSpec((B_WIDTH, B_CH), lambda bi, j: (0, 0)),
                  vec, vec, vec],
        out_specs=[pl.BlockSpec((1, tt, B_CH), lambda bi, j: (bi, j, 0)),
                   pl.BlockSpec((1, CONVB_HEAD, B_CH), lambda bi, j: (bi, 0, 0))],
        out_shape=[jax.ShapeDtypeStruct((b, t, B_CH), F32), jax.ShapeDtypeStruct((b, CONVB_HEAD, B_CH), F32)],
        scratch_shapes=[pltpu.VMEM((CONVB_HEAD + tt, B_CH), F32)],
        compiler_params=_cparams(("arbitrary", "arbitrary")),
        name="conv_module",
    )(proj, proj, init, w, dwb, lng, lnb)


GDN_HEAD = 8
GDN_BB = 4


def _unit_lower_inverses(mats):
    n = mats[0].shape[0]
    r = lax.broadcasted_iota(jnp.int32, (n, n), 0)
    c = lax.broadcasted_iota(jnp.int32, (n, n), 1)
    eye = jnp.where(r == c, 1.0, 0.0)
    xs = [-a for a in mats]
    ps = [eye + x for x in xs]
    splits = [_split_bf16(x) for x in xs]
    for _ in range(int(math.log2(n)) - 1):
        splits = [_split_bf16(_dot_split(s, s)) for s in splits]
        ps = [p + _dot_split(_split_bf16(p), s) for p, s in zip(ps, splits)]
    return ps


def _split_bf16(x):
    hi = x.astype(BF16)
    return hi, (x - hi.astype(F32)).astype(BF16)


def _dot_split(a, b):
    (ah, al), (bh, bl) = a, b
    return _dot(ah, bh) + (_dot(ah, bl) + _dot(al, bh))


def _gdn_kernel(x_ref, z_ref, misc_ref, cinit_ref, s0_ref, cw_ref, alog_ref, dtb_ref, ng_ref,
                o_ref, tail_ref, sout_ref, f_ref, s_ref):
    c = pl.program_id(1)

    @pl.when(c == 0)
    def _():
        f_ref[:, 0:GDN_HEAD, :] = cinit_ref[...]
        s_ref[...] = s0_ref[...]

    bb, cc = x_ref.shape[0], x_ref.shape[1]
    lead = GDN_HEAD - (D_CONV - 1)
    r = lax.broadcasted_iota(jnp.int32, (cc, cc), 0)
    col = lax.broadcasted_iota(jnp.int32, (cc, cc), 1)
    incl = col <= r
    strict = col < r
    lower = jnp.where(incl, 1.0, 0.0)

    ys, betas, gcums, gcum_ts = [], [], [], []
    for bi in range(bb):
        u = x_ref[bi]
        f_ref[bi, GDN_HEAD:GDN_HEAD + cc, :] = u
        y = None
        for j in range(D_CONV):
            term = cw_ref[j:j + 1, :] * f_ref[bi, lead + j:lead + j + cc, :]
            y = term if y is None else y + term
        ys.append(_silu(y))
        last = u[cc - GDN_HEAD:cc, :]
        tail_ref[bi] = last
        f_ref[bi, 0:GDN_HEAD, :] = last
        misc = misc_ref[bi]
        betas.append(jax.nn.sigmoid(misc))
        xg = misc + dtb_ref[...]
        softplus = jnp.maximum(xg, 0.0) + jnp.log(1.0 + jnp.exp(-jnp.abs(xg)))
        g_all = -jnp.exp(alog_ref[...]) * softplus
        gcum = _dot(lower, g_all, HIGHEST)
        gcums.append(gcum)
        gcum_ts.append(jnp.transpose(gcum))

    prob = [(bi, h) for bi in range(bb) for h in range(D_HEADS)]
    qn, kn, vh, bc, gc, gl, decay = [], [], [], [], [], [], []
    for bi, h in prob:
        y = ys[bi]
        q = y[:, h * D_KDIM:(h + 1) * D_KDIM]
        k = y[:, D_QK + h * D_KDIM:D_QK + (h + 1) * D_KDIM]
        vh.append(y[:, 2 * D_QK + h * D_VDIM:2 * D_QK + (h + 1) * D_VDIM])
        qn.append(q * lax.rsqrt(jnp.sum(q * q, axis=-1, keepdims=True) + EPS) * (D_KDIM ** -0.5))
        kn.append(k * lax.rsqrt(jnp.sum(k * k, axis=-1, keepdims=True) + EPS))
        bc.append(betas[bi][:, MISC_DB + h:MISC_DB + h + 1])
        g_col = gcums[bi][:, MISC_DA + h:MISC_DA + h + 1]
        g_row = gcum_ts[bi][MISC_DA + h:MISC_DA + h + 1, :]
        gc.append(g_col)
        gl.append(gcums[bi][cc - 1:cc, MISC_DA + h:MISC_DA + h + 1])
        decay.append(jnp.where(incl, jnp.exp(jnp.where(incl, g_col - g_row, 0.0)), 0.0))
    n = len(prob)
    kb = [x.astype(BF16) for x in kn]
    qb = [x.astype(BF16) for x in qn]
    kk = [_dot_nt(kb[g], kb[g]) for g in range(n)]
    tinv = _unit_lower_inverses([jnp.where(strict, bc[g] * decay[g] * kk[g], 0.0) for g in range(n)])
    s = [s_ref[bi, h] for bi, h in prob]
    sb = [x.astype(BF16) for x in s]
    eg = [jnp.exp(x) for x in gc]
    ks = [_dot(kb[g], sb[g]) for g in range(n)]
    uu = [_dot_split(_split_bf16(tinv[g]), _split_bf16(bc[g] * (vh[g] - eg[g] * ks[g]))) for g in range(n)]
    ub = [x.astype(BF16) for x in uu]
    qk = [(_dot_nt(qb[g], kb[g]) * decay[g]).astype(BF16) for g in range(n)]
    qs = [_dot(qb[g], sb[g]) for g in range(n)]
    o = [eg[g] * qs[g] + _dot(qk[g], ub[g]) for g in range(n)]
    kd = [(kn[g] * jnp.exp(gl[g] - gc[g])).astype(BF16) for g in range(n)]
    s_new = [jnp.exp(gl[g]) * s[g] + _dot_tn(kd[g], ub[g]) for g in range(n)]
    for g, (bi, h) in enumerate(prob):
        s_ref[bi, h] = s_new[g]
        zh = z_ref[bi, :, h * D_VDIM:(h + 1) * D_VDIM]
        o_ref[bi, :, h * D_VDIM:(h + 1) * D_VDIM] = _rms(o[g], ng_ref[...]) * _silu(zh)

    @pl.when(c == pl.num_programs(1) - 1)
    def _():
        sout_ref[...] = s_ref[...]


def gated_delta(proj, cinit, s0, cw, alog, dtb, ng):
    b, t, _ = proj.shape
    cc = min(t, CHUNK)
    bb = math.gcd(b, GDN_BB)
    vec = pl.BlockSpec((1, LANES), lambda bi, c: (0, 0))
    return pl.pallas_call(
        _gdn_kernel,
        grid=(b // bb, t // cc),
        in_specs=[pl.BlockSpec((bb, cc, D_CONV_CH), lambda bi, c: (bi, c, COL_DQKV // D_CONV_CH)),
                  pl.BlockSpec((bb, cc, D_V), lambda bi, c: (bi, c, COL_DZ // D_V)),
                  pl.BlockSpec((bb, cc, LANES), lambda bi, c: (bi, c, COL_MISC // LANES)),
                  pl.BlockSpec((bb, GDN_HEAD, D_CONV_CH), lambda bi, c: (bi, 0, 0)),
                  pl.BlockSpec((bb, D_HEADS, D_KDIM, D_VDIM), lambda bi, c: (bi, 0, 0, 0)),
                  pl.BlockSpec((D_CONV, D_CONV_CH), lambda bi, c: (0, 0)),
                  vec, vec, vec],
        out_specs=[pl.BlockSpec((bb, cc, D_V), lambda bi, c: (bi, c, 0)),
                   pl.BlockSpec((bb, GDN_HEAD, D_CONV_CH), lambda bi, c: (bi, 0, 0)),
                   pl.BlockSpec((bb, D_HEADS, D_KDIM, D_VDIM), lambda bi, c: (bi, 0, 0, 0))],
        out_shape=[jax.ShapeDtypeStruct((b, t, D_V), F32),
                   jax.ShapeDtypeStruct((b, GDN_HEAD, D_CONV_CH), F32),
                   jax.ShapeDtypeStruct((b, D_HEADS, D_KDIM, D_VDIM), F32)],
        scratch_shapes=[pltpu.VMEM((bb, GDN_HEAD + cc, D_CONV_CH), F32),
                        pltpu.VMEM((bb, D_HEADS, D_KDIM, D_VDIM), F32)],
        compiler_params=_cparams(("arbitrary", "arbitrary")),
        name="gated_delta",
    )(proj, proj, proj, cinit, s0, cw, alog, dtb, ng)


def _merge_kernel(oa_ref, ob_ref, oc_ref, od_ref, g0_ref, g1_ref, g2_ref, g3_ref, x_ref, gt_ref, ng_ref,
                  wbr_ref, wout_ref, o_ref):
    bt, tt, d = x_ref.shape
    rows = bt * tt
    merged = None
    for m, (br, gate) in enumerate(((oa_ref, g0_ref), (ob_ref, g1_ref), (oc_ref, g2_ref), (od_ref, g3_ref))):
        term = jax.nn.sigmoid(gate[...].reshape(rows, d)) * _dot(br[...].reshape(rows, BRANCH_W).astype(BF16), wbr_ref[m])
        merged = term if merged is None else merged + term
    mix = _dot(merged.astype(BF16), wout_ref[...])
    o_ref[...] = x_ref[...] + gt_ref[...] * _rms(mix, ng_ref[...]).reshape(bt, tt, d)


def merge_out(branches, proj, x, mod, ng, wbr, wout, bt, tt):
    b, t, d = x.shape
    gb = COL_GATE // d
    br_spec = pl.BlockSpec((bt, tt, BRANCH_W), lambda i, j: (i, j, 0))
    gate_specs = [pl.BlockSpec((bt, tt, d), functools.partial(lambda i, j, m: (i, j, gb + m), m=m))
                  for m in range(N_BRANCH)]
    return pl.pallas_call(
        _merge_kernel,
        grid=(b // bt, t // tt),
        in_specs=[br_spec] * 4 + gate_specs + [
            pl.BlockSpec((bt, tt, d), lambda i, j: (i, j, 0)),
            pl.BlockSpec((bt, 1, d), lambda i, j: (i, 0, 2)),
            pl.BlockSpec((1, d), lambda i, j: (0, 0)),
            pl.BlockSpec((N_BRANCH, BRANCH_W, d), lambda i, j: (0, 0, 0)),
            pl.BlockSpec((d, d), lambda i, j: (0, 0))],
        out_specs=pl.BlockSpec((bt, tt, d), lambda i, j: (i, j, 0)),
        out_shape=jax.ShapeDtypeStruct((b, t, d), F32),
        compiler_params=_cparams(("arbitrary", "arbitrary")),
        name="merge_out",
    )(*branches, proj, proj, proj, proj, x, mod, ng, wbr, wout)


def _mlp_kernel(x_ref, sh_ref, sc_ref, gt_ref, g2_ref, g3_ref, w1_ref, w2_ref, o_ref, h_ref, acc_ref):
    bt, tt, d = x_ref.shape
    f = pl.program_id(2)

    @pl.when(f == 0)
    def _():
        h = _rms(x_ref[...], g2_ref[...]) * (1.0 + sc_ref[...]) + sh_ref[...]
        h_ref[...] = h.reshape(bt * tt, d).astype(BF16)
        acc_ref[...] = jnp.zeros(acc_ref.shape, F32)

    a = jnp.maximum(_dot(h_ref[...], w1_ref[...]), 0.0)
    acc_ref[...] += _dot((a * a).astype(BF16), w2_ref[...])

    @pl.when(f == pl.num_programs(2) - 1)
    def _():
        o_ref[...] = x_ref[...] + gt_ref[...] * _rms(acc_ref[...], g3_ref[...]).reshape(bt, tt, d)


def mlp(x, mod, g2, g3, w1, w2, bt, tt):
    b, t, d = x.shape
    ff = w1.shape[1]
    tf = 1024
    return pl.pallas_call(
        _mlp_kernel,
        grid=(b // bt, t // tt, ff // tf),
        in_specs=[pl.BlockSpec((bt, tt, d), lambda i, j, f: (i, j, 0)),
                  pl.BlockSpec((bt, 1, d), lambda i, j, f: (i, 0, 3)),
                  pl.BlockSpec((bt, 1, d), lambda i, j, f: (i, 0, 4)),
                  pl.BlockSpec((bt, 1, d), lambda i, j, f: (i, 0, 5)),
                  pl.BlockSpec((1, d), lambda i, j, f: (0, 0)),
                  pl.BlockSpec((1, d), lambda i, j, f: (0, 0)),
                  pl.BlockSpec((d, tf), lambda i, j, f: (0, f)),
                  pl.BlockSpec((tf, d), lambda i, j, f: (f, 0))],
        out_specs=pl.BlockSpec((bt, tt, d), lambda i, j, f: (i, j, 0)),
        out_shape=jax.ShapeDtypeStruct((b, t, d), F32),
        scratch_shapes=[pltpu.VMEM((bt * tt, d), BF16), pltpu.VMEM((bt * tt, d), F32)],
        compiler_params=_cparams(("arbitrary", "arbitrary", "arbitrary")),
        name="mlp",
    )(x, mod, mod, mod, g2, g3, w1, w2)


def _combined_in_weight(w_in_l, w_gate_l):
    offs = np.concatenate([[0], np.cumsum(IN_SIZES)])
    (aq, ak, av, bglu, cq, ck, cv, iq, ik, iw, dqkv, dz, db, da) = [
        w_in_l[:, int(offs[i]):int(offs[i + 1])] for i in range(len(IN_SIZES))]
    d = w_in_l.shape[0]
    misc = jnp.concatenate([ik, iw, db, da, jnp.zeros((d, LANES - IDX_DIM - IDX_HEADS - 2 * D_HEADS), F32)], axis=1)
    pad = jnp.zeros((d, COL_GATE - COL_MISC - LANES), F32)
    gates = [w_gate_l[m] for m in range(N_BRANCH)]
    w = jnp.concatenate([aq, ak, av, bglu, cq, ck, cv, dqkv, dz, iq, misc, pad] + gates, axis=1)
    assert w.shape[1] == N_PROJ
    return w.astype(BF16)


def _lane_pad(v, offset):
    return jnp.zeros((1, LANES), F32).at[0, offset:offset + v.shape[0]].set(v)


def _run_group(x, mod, lw, past, bias, rel_bias, tiles):
    b, t, d = x.shape
    bt, tt, conv_tt = tiles
    proj = inproj(x, mod, lw['g'][0:1], lw['w_all'], bt, tt)

    if past is None:
        b_init = jnp.zeros((b, CONVB_HEAD, B_CH), F32)
        d_init = jnp.zeros((b, GDN_HEAD, D_CONV_CH), F32)
        s0 = jnp.zeros((b, D_HEADS, D_KDIM, D_VDIM), F32)
        o_a = flash_causal(proj, bias, rel_bias, lw['lam_init'], diff=True, qcol=COL_AQ, kcol=COL_AK, vcol=COL_AV,
                           bias_col0=0, a_lambda=lw['a_lambda'], a_norm_g=lw['a_norm_g'])
        sel = select_causal(proj)
        o_c = flash_causal(proj, bias, rel_bias, lw['lam_init'], diff=False, qcol=COL_CQ, kcol=COL_CK, vcol=COL_CV,
                           bias_col0=A_HEADS, mask=sel)
    else:
        layer = past['layer']
        lead_b = CONVB_HEAD - (B_WIDTH - 1)
        b_init = jnp.pad(past['b_conv'][layer], ((0, 0), (lead_b, 0), (0, 0)))
        d_init = jnp.pad(past['d_conv'][layer], ((0, 0), (GDN_HEAD - (D_CONV - 1), 0), (0, 0)))
        s0 = past['d_state'][layer]
        o_a = flash_full(proj, past['a_k'], past['a_v'], layer, bias, lw['lam_init'], diff=True,
                         qcol=COL_AQ, kcol=COL_AK, vcol=COL_AV, bias_col0=0,
                         a_lambda=lw['a_lambda'], a_norm_g=lw['a_norm_g'])
        sel = select_full(proj, past['c_kidx'], layer)
        o_c = flash_full(proj, past['c_k'], past['c_v'], layer, bias, lw['lam_init'], diff=False,
                         qcol=COL_CQ, kcol=COL_CK, vcol=COL_CV, bias_col0=A_HEADS, mask=sel)

    o_b, b_tail = conv_module(proj, b_init, lw['b_dw_w'], lw['b_dw_b'], lw['b_ln_g'], lw['b_ln_b'], conv_tt)
    o_d, d_tail, s_new = gated_delta(proj, d_init, s0, lw['d_conv_w'], lw['alog'], lw['dtb'], lw['d_norm_g'])

    mbt, mtt = (1, 256) if t >= 256 else (min(b, 256 // t), t)
    x1 = merge_out((o_a, o_b, o_c, o_d), proj, x, mod, lw['g'][1:2], lw['w_br'], lw['w_out'], mbt, mtt)
    fbt, ftt = (1, 512) if t >= 512 else (min(b, 512 // t), t)
    x2 = mlp(x1, mod, lw['g'][2:3], lw['g'][3:4], lw['w1'], lw['w2'], fbt, ftt)

    new = {
        'a_k': proj[:, :, COL_AK:COL_AK + A_QK].reshape(b, t, A_HEADS, 2 * A_DIM),
        'a_v': proj[:, :, COL_AV:COL_AV + A_QK].reshape(b, t, A_HEADS, 2 * A_DIM),
        'c_k': proj[:, :, COL_CK:COL_CK + C_W].reshape(b, t, C_HEADS, C_DIM),
        'c_v': proj[:, :, COL_CV:COL_CV + C_W].reshape(b, t, C_HEADS, C_DIM),
        'c_kidx': proj[:, :, COL_MISC:COL_MISC + IDX_DIM],
        'b_conv': b_tail[:, CONVB_HEAD - (B_WIDTH - 1):, :],
        'd_conv': d_tail[:, GDN_HEAD - (D_CONV - 1):, :],
        'd_state': s_new,
    }
    return x2, new


def kernel(x_prompt, x_sample, c_prompt, c_sample, cache_a_k, cache_a_v, cache_c_k, cache_c_v, cache_c_kidx, state_b_conv, state_d_conv, state_d_state, rel_bias, ada_w, ada_b, norm_g, w_in, a_lambda, a_norm_g, b_dw_w, b_dw_b, b_ln_g, b_ln_b, d_conv_w, d_a_log, d_dt_bias, d_norm_g, w_gate, w_br, w_out, mlp_w1, mlp_w2):
    depth = w_in.shape[0]
    bp, tp, d = x_prompt.shape
    bs, ts, _ = x_sample.shape
    past_len = cache_a_k.shape[2]

    mod_all = adaln_mod(jnp.concatenate([c_prompt, c_sample], axis=0), ada_w, ada_b)
    bias_p = bias_tiles(rel_bias, FLASH_T, 2 * FLASH_T, FLASH_T, 0)
    bias_s = bias_tiles(rel_bias, ts, past_len + ts, past_len, 0)

    past = {
        'a_k': cache_a_k.reshape(depth, bs, past_len, A_QK),
        'a_v': cache_a_v.reshape(depth, bs, past_len, A_QK),
        'c_k': cache_c_k.reshape(depth, bs, past_len, C_W),
        'c_v': cache_c_v.reshape(depth, bs, past_len, C_W),
        'c_kidx': cache_c_kidx, 'b_conv': state_b_conv, 'd_conv': state_d_conv, 'd_state': state_d_state,
    }
    names = ('a_k', 'a_v', 'c_k', 'c_v', 'c_kidx', 'b_conv', 'd_conv', 'd_state')
    new_p = {n: [] for n in names}
    new_s = {n: [] for n in names}
    xp, xs = x_prompt, x_sample
    for l in range(depth):
        lw = {
            'g': norm_g[l],
            'w_all': _combined_in_weight(w_in[l], w_gate[l]),
            'lam_init': jnp.full((1,), 0.8 - 0.6 * math.exp(-0.3 * l), F32),
            'a_lambda': a_lambda[l],
            'a_norm_g': a_norm_g[l].reshape(1, LANES),
            'b_dw_w': b_dw_w[l], 'b_dw_b': b_dw_b[l].reshape(1, B_CH),
            'b_ln_g': b_ln_g[l].reshape(1, B_CH), 'b_ln_b': b_ln_b[l].reshape(1, B_CH),
            'd_conv_w': d_conv_w[l],
            'alog': _lane_pad(d_a_log[l], MISC_DA), 'dtb': _lane_pad(d_dt_bias[l], MISC_DA),
            'd_norm_g': d_norm_g[l].reshape(1, LANES),
            'w_br': w_br[l].astype(BF16), 'w_out': w_out[l].astype(BF16),
            'w1': mlp_w1[l].astype(BF16), 'w2': mlp_w2[l].astype(BF16),
        }
        mod_p = mod_all[l, :bp].reshape(bp, 1, 6 * d)
        mod_s = mod_all[l, bp:].reshape(bs, 1, 6 * d)
        xp, sp = _run_group(xp, mod_p, lw, None, bias_p, rel_bias, (1, 1024, 256))
        xs, ss = _run_group(xs, mod_s, lw, dict(past, layer=l), bias_s, rel_bias, (min(bs, 1024 // ts), ts, ts))
        for n in names:
            new_p[n].append(sp[n])
            new_s[n].append(ss[n])
    return (xp, xs) + tuple(jnp.stack(new_p[n]) for n in names) + tuple(jnp.stack(new_s[n]) for n in names)
```

```python
import functools
import math

import numpy as np
import jax
import jax.numpy as jnp
from jax import lax
from jax.experimental import pallas as pl
from jax.experimental.pallas import tpu as pltpu

F32 = jnp.float32
BF16 = jnp.bfloat16
HIGHEST = lax.Precision.HIGHEST

D_MODEL = 1024
CHUNK = 64
BRANCH_W = D_MODEL // 2
N_BRANCH = 4
A_HEADS = 4
A_DIM = 64
B_CH = BRANCH_W
B_WIDTH = 31
C_HEADS = 8
C_DIM = 64
IDX_HEADS = 4
IDX_DIM = 64
TOPK_MAX = 256
D_HEADS = 4
D_VDIM = 128
D_KDIM = 64
D_CONV = 4
NUM_BUCKETS = 32
MAX_DISTANCE = 128
D_FF = 4 * D_MODEL
EPS = 1e-6
A_QK = A_HEADS * 2 * A_DIM
C_W = C_HEADS * C_DIM
D_QK = D_HEADS * D_KDIM
D_V = D_HEADS * D_VDIM
D_CONV_CH = 2 * D_QK + D_V
IN_SIZES = (A_QK, A_QK, A_QK, 2 * B_CH, C_W, C_W, C_W, IDX_HEADS * IDX_DIM, IDX_DIM, IDX_HEADS,
            D_CONV_CH, D_V, D_HEADS, D_HEADS)

LANES = 128
NEG = -1e30
LOG2E = 1.4426950408889634
INT_MIN = -2 ** 31
VMEM_LIMIT = 56 * 1024 * 1024

COL_AQ, COL_AK, COL_AV = 0, 512, 1024
COL_BGLU = 1536
COL_CQ, COL_CK, COL_CV = 2560, 3072, 3584
COL_DQKV = 4096
COL_DZ = 5120
COL_IQ = 5632
COL_MISC = 5888
COL_GATE = 6144
N_PROJ = 10240
MISC_IW, MISC_DB, MISC_DA = 64, 68, 72

FLASH_T = 256
FLASH_FAR = 1024
FLASH_PAIRS = 2
SEL_TQ = 256
SEL_KB = 256
SEL_ROWS = 512
SEL_PRE = 256
I16_MIN = -2 ** 15


def _cparams(sem):
    return pltpu.CompilerParams(dimension_semantics=sem, vmem_limit_bytes=VMEM_LIMIT)


def _dot(a, b, precision=None):
    return jnp.dot(a, b, preferred_element_type=F32, precision=precision)


def _dot_nt(a, b):
    return lax.dot_general(a, b, (((1,), (1,)), ((), ())), preferred_element_type=F32)


def _dot_tn(a, b):
    return lax.dot_general(a, b, (((0,), (0,)), ((), ())), preferred_element_type=F32)


def _rms(x, g):
    return x * lax.rsqrt(jnp.mean(x * x, axis=-1, keepdims=True) + EPS) * g


def _silu(x):
    return x * jax.nn.sigmoid(x)


def _chunk_of(pos):
    return jnp.right_shift(pos, int(math.log2(CHUNK)))


def _mod_kernel(c_ref, w_ref, b_ref, o_ref):
    s = _silu(c_ref[...])
    o_ref[0] = _dot(s.astype(BF16), w_ref[0].astype(BF16)) + b_ref[0]


def adaln_mod(c_all, ada_w, ada_b):
    depth, d, n = ada_w.shape
    bc = c_all.shape[0]
    tn = 1024
    return pl.pallas_call(
        _mod_kernel,
        grid=(depth, n // tn),
        in_specs=[pl.BlockSpec((bc, d), lambda l, j: (0, 0)),
                  pl.BlockSpec((1, d, tn), lambda l, j: (l, 0, j)),
                  pl.BlockSpec((1, 1, tn), lambda l, j: (l, 0, j))],
        out_specs=pl.BlockSpec((1, bc, tn), lambda l, j: (l, 0, j)),
        out_shape=jax.ShapeDtypeStruct((depth, bc, n), F32),
        compiler_params=_cparams(("arbitrary", "arbitrary")),
        name="adaln_mod",
    )(c_all, ada_w, ada_b.reshape(depth, 1, n))


def _bias_kernel(tab_ref, o_ref, *, q0, k0):
    h = pl.program_id(0)
    tq, w = o_ref.shape[1], o_ref.shape[2]
    row = lax.broadcasted_iota(jnp.int32, (tq, w), 0)
    col = lax.broadcasted_iota(jnp.int32, (tq, w), 1)
    rel = (col + k0) - (row + q0)
    nb = NUM_BUCKETS // 2
    max_exact = nb // 2
    n = jnp.abs(rel)
    large = max_exact + (jnp.log(jnp.maximum(n, max_exact).astype(F32) / max_exact)
                         / math.log(MAX_DISTANCE / max_exact) * (nb - max_exact)).astype(jnp.int32)
    large = jnp.minimum(large, nb - 1)
    bucket = jnp.where(rel > 0, nb, 0) + jnp.where(n < max_exact, n, large)
    val = jnp.zeros((tq, w), F32)
    for bk in range(NUM_BUCKETS):
        val = jnp.where(bucket == bk, tab_ref[bk, h], val)
    o_ref[0] = val * LOG2E


def bias_tiles(rel_bias, tq, w, q0, k0):
    nh = rel_bias.shape[1]
    return pl.pallas_call(
        functools.partial(_bias_kernel, q0=q0, k0=k0),
        grid=(nh,),
        in_specs=[pl.BlockSpec(memory_space=pltpu.SMEM)],
        out_specs=pl.BlockSpec((1, tq, w), lambda h: (h, 0, 0)),
        out_shape=jax.ShapeDtypeStruct((nh, tq, w), F32),
        compiler_params=_cparams(("arbitrary",)),
        name="bias_tiles",
    )(rel_bias)


def _inproj_kernel(x_ref, sh_ref, sc_ref, g_ref, w_ref, o_ref, h_ref):
    bt, tt, d = x_ref.shape

    @pl.when(pl.program_id(2) == 0)
    def _():
        h = _rms(x_ref[...], g_ref[...]) * (1.0 + sc_ref[...]) + sh_ref[...]
        h_ref[...] = h.reshape(bt * tt, d).astype(BF16)

    o_ref[...] = _dot(h_ref[...], w_ref[...]).reshape(o_ref.shape)


def inproj(x, mod, g, w_all, bt, tt):
    b, t, d = x.shape
    n = w_all.shape[1]
    tn = 1024
    return pl.pallas_call(
        _inproj_kernel,
        grid=(b // bt, t // tt, n // tn),
        in_specs=[pl.BlockSpec((bt, tt, d), lambda i, j, k: (i, j, 0)),
                  pl.BlockSpec((bt, 1, d), lambda i, j, k: (i, 0, 0)),
                  pl.BlockSpec((bt, 1, d), lambda i, j, k: (i, 0, 1)),
                  pl.BlockSpec((1, d), lambda i, j, k: (0, 0)),
                  pl.BlockSpec((d, tn), lambda i, j, k: (0, k))],
        out_specs=pl.BlockSpec((bt, tt, tn), lambda i, j, k: (i, j, k)),
        out_shape=jax.ShapeDtypeStruct((b, t, n), F32),
        scratch_shapes=[pltpu.VMEM((bt * tt, d), BF16)],
        compiler_params=_cparams(("arbitrary", "arbitrary", "arbitrary")),
        name="inproj",
    )(x, mod, mod, g, w_all)


def _flash_tile(qs, kbs, vbs, biases, shifts, add_mask, keep, m_ref, l_ref, acc_ref):
    scores = [_dot_nt(q, kbs[m // 2]) for m, q in enumerate(qs)]
    for m, s in enumerate(scores):
        if biases[m] is not None:
            s = s + biases[m]
        if add_mask is not None:
            s = s + add_mask
        if keep is not None:
            s = jnp.where(keep, s, NEG)
        m_prev = m_ref[m]
        s_max = jnp.max(s, axis=-1, keepdims=True)
        if shifts[m] is not None:
            s_max = s_max + shifts[m]
        m_new = jnp.maximum(m_prev, s_max)
        alpha = jnp.exp2(m_prev - m_new)
        p = jnp.exp2(s - (m_new if shifts[m] is None else m_new - shifts[m]))
        l_ref[m] = alpha * l_ref[m] + jnp.sum(p, axis=-1, keepdims=True)
        acc_ref[m] = alpha * acc_ref[m] + _dot(p.astype(BF16), vbs[m // 2])
        m_ref[m] = m_new


def _flash_finish(diff, lam_ref, alam_ref, g_ref, o_ref, l_ref, acc_ref, lane):
    for j in range(o_ref.shape[2] // LANES):
        o0 = acc_ref[2 * j] / l_ref[2 * j]
        o1 = acc_ref[2 * j + 1] / l_ref[2 * j + 1]
        if diff:
            lv = alam_ref[...]
            lam_init = lam_ref[0]
            lam = (jnp.exp(jnp.sum(lv[0:1] * lv[1:2], axis=-1, keepdims=True))
                   - jnp.exp(jnp.sum(lv[2:3] * lv[3:4], axis=-1, keepdims=True)) + lam_init)
            o = o0 - lam * o1
            o = _rms(o, g_ref[...]) * (1.0 - lam_init)
        else:
            o = jnp.where(lane < C_DIM, o0, o1)
        o_ref[0, :, j * LANES:(j + 1) * LANES] = o


def _flash_init(q_ref, m_ref, l_ref, acc_ref):
    lane = lax.broadcasted_iota(jnp.int32, (1, LANES), 1)
    qs = []
    for j in range(q_ref.shape[2] // LANES):
        q = q_ref[0, :, j * LANES:(j + 1) * LANES] * (A_DIM ** -0.5 * LOG2E)
        qs.append(jnp.where(lane < A_DIM, q, 0.0).astype(BF16))
        qs.append(jnp.where(lane >= A_DIM, q, 0.0).astype(BF16))
    m_ref[...] = jnp.full(m_ref.shape, NEG, F32)
    l_ref[...] = jnp.zeros(l_ref.shape, F32)
    acc_ref[...] = jnp.zeros(acc_ref.shape, F32)
    return lane, qs


def _flash_causal_kernel(*refs, diff, has_mask, col0):
    tab_ref, lam_ref, q_ref, k_ref, v_ref, bias_ref = refs[:6]
    pos = 6
    mask_ref = None
    if has_mask:
        mask_ref = refs[pos]
        pos += 1
    alam_ref = g_ref = None
    if diff:
        alam_ref, g_ref = refs[pos], refs[pos + 1]
        pos += 2
    o_ref, m_ref, l_ref, acc_ref = refs[pos:pos + 4]
    t = q_ref.shape[1]
    npair = q_ref.shape[2] // LANES
    nb = bias_ref.shape[0] // npair
    p = pl.program_id(1)
    i = pl.program_id(2)
    lane, qs = _flash_init(q_ref, m_ref, l_ref, acc_ref)
    bias_of = [nb * (m // 2) + min(m % 2, nb - 1) for m in range(2 * npair)]
    none = (None,) * (2 * npair)
    far = [tab_ref[NUM_BUCKETS // 2 - 1, col0 + nb * npair * p + bias_of[m]] * LOG2E for m in range(2 * npair)]

    def tile(ks, width, biases, shifts, keep):
        kbs = [k_ref[0, pl.ds(ks, width), j * LANES:(j + 1) * LANES].astype(BF16) for j in range(npair)]
        vbs = [v_ref[0, pl.ds(ks, width), j * LANES:(j + 1) * LANES].astype(BF16) for j in range(npair)]
        add = mask_ref[0, :, pl.ds(ks, width)].astype(F32) if has_mask else None
        _flash_tile(qs, kbs, vbs, biases, shifts, add, keep, m_ref, l_ref, acc_ref)

    n_far = jnp.maximum(i - 1, 0)
    wide = FLASH_FAR // t
    n_wide = n_far // wide

    def wide_body(kj, carry):
        tile(pl.multiple_of(kj * FLASH_FAR, FLASH_FAR), FLASH_FAR, none, far, None)
        return carry

    lax.fori_loop(0, n_wide, wide_body, 0)
    width = FLASH_FAR // 2
    done = n_wide * wide
    while width >= t:
        has = ((n_far - done) * t) >= width
        start = pl.multiple_of(done * t, t)

        @pl.when(has)
        def _(start=start, width=width):
            tile(start, width, none, far, None)

        done = done + jnp.where(has, width // t, 0)
        width //= 2

    row = lax.broadcasted_iota(jnp.int32, (t, 2 * t), 0)
    col = lax.broadcasted_iota(jnp.int32, (t, 2 * t), 1)
    keep = _chunk_of(col) <= _chunk_of(row) + t // CHUNK

    @pl.when(i == 0)
    def _():
        tile(0, t, [bias_ref[k, :, t:2 * t] for k in bias_of], none, keep[:, t:2 * t])

    @pl.when(i >= 1)
    def _():
        tile(pl.multiple_of((i - 1) * t, t), 2 * t, [bias_ref[k] for k in bias_of], none, keep)

    _flash_finish(diff, lam_ref, alam_ref, g_ref, o_ref, l_ref, acc_ref, lane)


def flash_causal(proj, near_bias, rel_bias, lam_init, *, diff, qcol, kcol, vcol, bias_col0,
                 mask=None, a_lambda=None, a_norm_g=None):
    b, t_all, _ = proj.shape
    t = FLASH_T
    npair = 4
    pp = FLASH_PAIRS
    w = pp * LANES
    nb = (1 if diff else 2) * pp
    qb, kb, vb = qcol // w, kcol // w, vcol // w
    bb0 = bias_col0 // nb
    in_specs = [pl.BlockSpec(memory_space=pltpu.SMEM),
                pl.BlockSpec(memory_space=pltpu.SMEM),
                pl.BlockSpec((1, t, w), lambda bi, p, i: (bi, i, qb + p)),
                pl.BlockSpec((1, t_all, w), lambda bi, p, i: (bi, 0, kb + p)),
                pl.BlockSpec((1, t_all, w), lambda bi, p, i: (bi, 0, vb + p)),
                pl.BlockSpec((nb, t, 2 * t), lambda bi, p, i: (bb0 + p, 0, 0))]
    args = [rel_bias, lam_init, proj, proj, proj, near_bias]
    if mask is not None:
        in_specs.append(pl.BlockSpec((1, t, t_all), lambda bi, p, i: (bi, i, 0)))
        args.append(mask)
    if diff:
        in_specs += [pl.BlockSpec((4, A_DIM), lambda bi, p, i: (0, 0)),
                     pl.BlockSpec((1, LANES), lambda bi, p, i: (0, 0))]
        args += [a_lambda, a_norm_g]
    return pl.pallas_call(
        functools.partial(_flash_causal_kernel, diff=diff, has_mask=mask is not None, col0=bias_col0),
        grid=(b, npair // pp, t_all // t),
        in_specs=in_specs,
        out_specs=pl.BlockSpec((1, t, w), lambda bi, p, i: (bi, i, p)),
        out_shape=jax.ShapeDtypeStruct((b, t_all, npair * LANES), F32),
        scratch_shapes=[pltpu.VMEM((2 * pp, t, 1), F32), pltpu.VMEM((2 * pp, t, 1), F32),
                        pltpu.VMEM((2 * pp, t, LANES), F32)],
        compiler_params=_cparams(("arbitrary", "arbitrary", "arbitrary")),
        name="flash_diff" if diff else "flash_sel",
    )(*args)


def _flash_full_kernel(*refs, diff, has_mask, tk):
    lam_ref, q_ref, kc_ref, vc_ref, kn_ref, vn_ref, bias_ref = refs[:7]
    pos = 7
    mask_ref = None
    if has_mask:
        mask_ref = refs[pos]
        pos += 1
    alam_ref = g_ref = None
    if diff:
        alam_ref, g_ref = refs[pos], refs[pos + 1]
        pos += 2
    o_ref, m_ref, l_ref, acc_ref = refs[pos:pos + 4]
    past = kc_ref.shape[2]
    tn = kn_ref.shape[1]
    nb = bias_ref.shape[0]
    lane, qs = _flash_init(q_ref, m_ref, l_ref, acc_ref)

    def tile(kb, vb, c0, width):
        add = mask_ref[0, :, c0:c0 + width].astype(F32) if has_mask else None
        _flash_tile(qs, [kb], [vb], [bias_ref[min(m, nb - 1), :, c0:c0 + width] for m in range(2)], (None, None),
                    add, None, m_ref, l_ref, acc_ref)

    for j in range(past // tk):
        tile(kc_ref[0, 0, j * tk:(j + 1) * tk, :].astype(BF16), vc_ref[0, 0, j * tk:(j + 1) * tk, :].astype(BF16),
             j * tk, tk)
    tile(kn_ref[0].astype(BF16), vn_ref[0].astype(BF16), past, tn)
    _flash_finish(diff, lam_ref, alam_ref, g_ref, o_ref, l_ref, acc_ref, lane)


def flash_full(proj, k_cache, v_cache, layer, full_bias, lam_init, *, diff, qcol, kcol, vcol, bias_col0,
               mask=None, a_lambda=None, a_norm_g=None):
    b, tq, _ = proj.shape
    past = k_cache.shape[2]
    npair = 4
    nb = 1 if diff else 2
    ltot = past + tq
    qb, kb, vb = qcol // LANES, kcol // LANES, vcol // LANES
    bb0 = bias_col0 // nb
    in_specs = [pl.BlockSpec(memory_space=pltpu.SMEM),
                pl.BlockSpec((1, tq, LANES), lambda bi, p: (bi, 0, qb + p)),
                pl.BlockSpec((1, 1, past, LANES), lambda bi, p: (layer, bi, 0, p)),
                pl.BlockSpec((1, 1, past, LANES), lambda bi, p: (layer, bi, 0, p)),
                pl.BlockSpec((1, tq, LANES), lambda bi, p: (bi, 0, kb + p)),
                pl.BlockSpec((1, tq, LANES), lambda bi, p: (bi, 0, vb + p)),
                pl.BlockSpec((nb, tq, ltot), lambda bi, p: (bb0 + p, 0, 0))]
    args = [lam_init, proj, k_cache, v_cache, proj, proj, full_bias]
    if mask is not None:
        in_specs.append(pl.BlockSpec((1, tq, mask.shape[2]), lambda bi, p: (bi, 0, 0)))
        args.append(mask)
    if diff:
        in_specs += [pl.BlockSpec((4, A_DIM), lambda bi, p: (0, 0)),
                     pl.BlockSpec((1, LANES), lambda bi, p: (0, 0))]
        args += [a_lambda, a_norm_g]
    return pl.pallas_call(
        functools.partial(_flash_full_kernel, diff=diff, has_mask=mask is not None, tk=past),
        grid=(b, npair),
        in_specs=in_specs,
        out_specs=pl.BlockSpec((1, tq, LANES), lambda bi, p: (bi, 0, p)),
        out_shape=jax.ShapeDtypeStruct((b, tq, npair * LANES), F32),
        scratch_shapes=[pltpu.VMEM((2, tq, 1), F32), pltpu.VMEM((2, tq, 1), F32), pltpu.VMEM((2, tq, LANES), F32)],
        compiler_params=_cparams(("arbitrary", "arbitrary")),
        name="flash_diff_full" if diff else "flash_sel_full",
    )(*args)


def _index_operands(qi, misc, width):
    tq = qi.shape[0]
    w = misc[:, MISC_IW:MISC_IW + IDX_HEADS] * (IDX_HEADS ** -0.5 * IDX_DIM ** -0.5)
    qh = [qi[:, h * IDX_DIM:(h + 1) * IDX_DIM].astype(BF16) for h in range(IDX_HEADS)]
    wh = [jnp.broadcast_to(w[:, h:h + 1], (tq, width)) for h in range(IDX_HEADS)]
    return qh, wh


def _index_scores(qh, wh, kb):
    isc = None
    for q, w in zip(qh, wh):
        term = w[:, :kb.shape[0]] * jnp.maximum(_dot_nt(q, kb), 0.0)
        isc = term if isc is None else isc + term
    return isc + 0.0


def _sortable(x):
    bits = lax.bitcast_convert_type(x, jnp.int32)
    return jnp.where(bits < 0, bits ^ jnp.int32(0x7FFFFFFF), bits)


def _store_keys(keys_ref, hi_ref, cs, key):
    width = key.shape[1]
    keys_ref[:, pl.ds(cs, width)] = key
    hi_ref[:, pl.ds(cs, width)] = jnp.right_shift(key, 16).astype(jnp.int16)


def _select_topk(keys_ref, hi_ref, lo_ref, nkb, k_sel, out_ref):
    tq = keys_ref.shape[0]

    def block(ref, j):
        return ref[:, pl.ds(pl.multiple_of(j * SEL_KB, SEL_KB), SEL_KB)]

    def count_ge(ref, cand):
        cb = jnp.broadcast_to(cand, (tq, SEL_KB)).astype(jnp.int16)

        def body(j, acc):
            return acc + jnp.where(block(ref, j) >= cb, jnp.int16(1), jnp.int16(0))

        acc = lax.fori_loop(0, nkb, body, jnp.zeros((tq, SEL_KB), jnp.int16))
        return jnp.sum(acc.astype(jnp.int32), axis=-1, keepdims=True)

    def count_gt(ref, t):
        top = -I16_MIN - 1
        return jnp.where(t >= top, 0, count_ge(ref, jnp.minimum(t + 1, top)))

    def search(ref, base):
        zero = jnp.zeros((tq, 1), jnp.int32)
        t0 = jnp.where(base + count_ge(ref, zero) >= k_sel, zero, jnp.full((tq, 1), I16_MIN, jnp.int32))

        def bit_body(it, t):
            cand = t + jnp.left_shift(jnp.int32(1), 14 - it)
            return jnp.where(base + count_ge(ref, cand) >= k_sel, cand, t)

        return lax.fori_loop(0, 15, bit_body, t0)

    hi = search(hi_ref, jnp.zeros((tq, 1), jnp.int32))
    above = count_gt(hi_ref, hi)
    hib = jnp.broadcast_to(hi, (tq, SEL_KB))

    def lo_body(j, carry):
        key = block(keys_ref, j)
        lo = jnp.bitwise_and(key, 0xFFFF) + I16_MIN
        lo = jnp.where(jnp.right_shift(key, 16) == hib, lo, I16_MIN)
        lo_ref[:, pl.ds(pl.multiple_of(j * SEL_KB, SEL_KB), SEL_KB)] = lo.astype(jnp.int16)
        return carry

    lax.fori_loop(0, nkb, lo_body, 0)
    lo = search(lo_ref, above)
    thr = hi * 65536 + (lo - I16_MIN)
    need = k_sel - (above + count_gt(lo_ref, lo))
    need = jnp.where(thr == INT_MIN, 0, need).astype(F32)
    thrb = jnp.broadcast_to(thr, (tq, SEL_KB))
    needb = jnp.broadcast_to(need, (tq, SEL_KB))
    r = lax.broadcasted_iota(jnp.int32, (SEL_KB, 2 * SEL_KB), 0)
    c = lax.broadcasted_iota(jnp.int32, (SEL_KB, 2 * SEL_KB), 1)
    tri = jnp.where((r <= c) | (c >= SEL_KB), 1.0, 0.0).astype(BF16)

    def mask_body(j, offs):
        blk = block(keys_ref, j)
        eq = blk == thrb
        cnt = _dot(jnp.where(eq, 1.0, 0.0).astype(BF16), tri)
        sel = (blk > thrb) | (eq & (offs + cnt[:, :SEL_KB] <= needb))
        out_ref[0, :, pl.ds(pl.multiple_of(j * SEL_KB, SEL_KB), SEL_KB)] = jnp.where(sel, 0.0, NEG).astype(out_ref.dtype)
        return offs + cnt[:, SEL_KB:]

    lax.fori_loop(0, nkb, mask_body, jnp.zeros((tq, SEL_KB), F32))


def _tree_sum(parts):
    while len(parts) > 1:
        parts = [parts[k] + parts[k + 1] for k in range(0, len(parts) - 1, 2)] + parts[len(parts) & ~1:]
    return parts[0]


def _select_causal_kernel(iq_ref, mq_ref, mk_ref, o_ref, keys_ref, hi_ref, lo_ref, *, k_sel):
    tq = iq_ref.shape[1]
    i = pl.program_id(1)
    nkb = (i * tq) // SEL_ROWS + 1
    sub = 16
    qt = jnp.transpose(iq_ref[0])
    mt = jnp.transpose(mq_ref[0])
    qh = [qt[h * IDX_DIM:(h + 1) * IDX_DIM, :].astype(BF16) for h in range(IDX_HEADS)]
    wh = [mt[MISC_IW + h:MISC_IW + h + 1, :] * (IDX_HEADS ** -0.5 * IDX_DIM ** -0.5) for h in range(IDX_HEADS)]
    qpos = lax.broadcasted_iota(jnp.int32, (SEL_ROWS, tq), 1) + i * tq
    kpos = lax.broadcasted_iota(jnp.int32, (SEL_ROWS, tq), 0)

    def rows(j):
        return pl.ds(pl.multiple_of(j * SEL_ROWS, SEL_ROWS), SEL_ROWS)

    def score_body(j, carry):
        kb = mk_ref[0, rows(j), :][:, 0:IDX_DIM].astype(BF16)
        isc = _tree_sum([w * jnp.maximum(_dot(kb, q), 0.0) for q, w in zip(qh, wh)]) + 0.0
        vis = _chunk_of(kpos + j * SEL_ROWS) <= _chunk_of(qpos)
        key = jnp.where(vis, _sortable(isc), INT_MIN)
        keys_ref[rows(j), :] = key
        hi_ref[rows(j), :] = jnp.right_shift(key, 16).astype(jnp.int16)
        return carry

    lax.fori_loop(0, nkb, score_body, 0)

    def count_ge(ref, cand):
        cb = jnp.broadcast_to(cand, (sub, tq)).astype(jnp.int16)

        def body(j, acc):
            blk = ref[rows(j), :]
            hits = [jnp.where(blk[k * sub:(k + 1) * sub] >= cb, jnp.int16(1), jnp.int16(0))
                    for k in range(SEL_ROWS // sub)]
            return acc + _tree_sum(hits)

        acc = lax.fori_loop(0, nkb, body, jnp.zeros((sub, tq), jnp.int16))
        return jnp.sum(acc.astype(jnp.int32), axis=0, keepdims=True)

    def count_gt(ref, t):
        top = -I16_MIN - 1
        return jnp.where(t >= top, 0, count_ge(ref, jnp.minimum(t + 1, top)))

    def search(ref, base):
        zero = jnp.zeros((1, tq), jnp.int32)
        t0 = jnp.where(base + count_ge(ref, zero) >= k_sel, zero, jnp.full((1, tq), I16_MIN, jnp.int32))

        def bit_body(it, t):
            cand = t + jnp.left_shift(jnp.int32(1), 14 - it)
            return jnp.where(base + count_ge(ref, cand) >= k_sel, cand, t)

        return lax.fori_loop(0, 15, bit_body, t0)

    hi = search(hi_ref, jnp.zeros((1, tq), jnp.int32))
    above = count_gt(hi_ref, hi)

    def lo_body(j, carry):
        key = keys_ref[rows(j), :]
        lo = jnp.bitwise_and(key, 0xFFFF) + I16_MIN
        lo = jnp.where(jnp.right_shift(key, 16) == hi, lo, I16_MIN)
        lo_ref[rows(j), :] = lo.astype(jnp.int16)
        return carry

    lax.fori_loop(0, nkb, lo_body, 0)
    lo = search(lo_ref, above)
    thr = hi * 65536 + (lo - I16_MIN)
    need = k_sel - (above + count_gt(lo_ref, lo))
    need = jnp.where(thr == INT_MIN, 0, need).astype(F32)
    r = lax.broadcasted_iota(jnp.int32, (2 * SEL_PRE, SEL_PRE), 0)
    c = lax.broadcasted_iota(jnp.int32, (2 * SEL_PRE, SEL_PRE), 1)
    tri = jnp.where((c <= r) | (r >= SEL_PRE), 1.0, 0.0).astype(BF16)
    r = lax.broadcasted_iota(jnp.int32, (tq, tq), 0)
    c = lax.broadcasted_iota(jnp.int32, (tq, tq), 1)
    eye = jnp.where(r == c, 1.0, 0.0).astype(BF16)

    def mask_body(j, offs):
        blk = keys_ref[rows(j), :]
        parts = [blk[k * SEL_PRE:(k + 1) * SEL_PRE] for k in range(SEL_ROWS // SEL_PRE)]
        eqs = [p == thr for p in parts]
        cnts = [_dot(tri, jnp.where(eq, 1.0, 0.0).astype(BF16)) for eq in eqs]
        sels = []
        for p, eq, cnt in zip(parts, eqs, cnts):
            sel = (p > thr) | (eq & (offs + cnt[:SEL_PRE] <= need))
            sels.append(jnp.where(sel, 1.0, 0.0).astype(BF16))
            offs = offs + cnt[SEL_PRE:SEL_PRE + 1]
        picked = _dot_nt(eye, jnp.concatenate(sels, axis=0))
        o_ref[0, :, rows(j)] = ((picked - 1.0) * -NEG).astype(o_ref.dtype)
        return offs

    lax.fori_loop(0, nkb, mask_body, jnp.zeros((1, tq), F32))


def select_causal(proj):
    b, t, _ = proj.shape
    tq = SEL_TQ
    k_sel = min(TOPK_MAX, t // 4)
    return pl.pallas_call(
        functools.partial(_select_causal_kernel, k_sel=k_sel),
        grid=(b, t // tq),
        in_specs=[pl.BlockSpec((1, tq, IDX_HEADS * IDX_DIM), lambda bi, i: (bi, i, COL_IQ // (IDX_HEADS * IDX_DIM))),
                  pl.BlockSpec((1, tq, LANES), lambda bi, i: (bi, i, COL_MISC // LANES)),
                  pl.BlockSpec((1, t, LANES), lambda bi, i: (bi, 0, COL_MISC // LANES))],
        out_specs=pl.BlockSpec((1, tq, t), lambda bi, i: (bi, i, 0)),
        out_shape=jax.ShapeDtypeStruct((b, t, t), BF16),
        scratch_shapes=[pltpu.VMEM((t, tq), jnp.int32), pltpu.VMEM((t, tq), jnp.int16),
                        pltpu.VMEM((t, tq), jnp.int16)],
        compiler_params=_cparams(("arbitrary", "arbitrary")),
        name="select_causal",
    )(proj, proj, proj)


def _select_full_kernel(iq_ref, mq_ref, kc_ref, o_ref, keys_ref, hi_ref, lo_ref, *, k_sel):
    tq = iq_ref.shape[1]
    past = kc_ref.shape[2]
    nkb = keys_ref.shape[1] // SEL_KB
    misc = mq_ref[0]
    qh, wh = _index_operands(iq_ref[0], misc, SEL_KB)
    for j in range(past // SEL_KB):
        kb = kc_ref[0, 0, j * SEL_KB:(j + 1) * SEL_KB, :].astype(BF16)
        _store_keys(keys_ref, hi_ref, j * SEL_KB, _sortable(_index_scores(qh, wh, kb)))
    _store_keys(keys_ref, hi_ref, past, jnp.full((tq, keys_ref.shape[1] - past), INT_MIN, jnp.int32))
    _store_keys(keys_ref, hi_ref, past, _sortable(_index_scores(qh, wh, misc[:, 0:IDX_DIM].astype(BF16))))
    _select_topk(keys_ref, hi_ref, lo_ref, nkb, k_sel, o_ref)


def select_full(proj, kidx_cache, layer):
    b, tq, _ = proj.shape
    past = kidx_cache.shape[2]
    ltot = past + tq
    lpad = -(-ltot // SEL_KB) * SEL_KB
    k_sel = min(TOPK_MAX, ltot // 4)
    return pl.pallas_call(
        functools.partial(_select_full_kernel, k_sel=k_sel),
        grid=(b,),
        in_specs=[pl.BlockSpec((1, tq, IDX_HEADS * IDX_DIM), lambda bi: (bi, 0, COL_IQ // (IDX_HEADS * IDX_DIM))),
                  pl.BlockSpec((1, tq, LANES), lambda bi: (bi, 0, COL_MISC // LANES)),
                  pl.BlockSpec((1, 1, past, IDX_DIM), lambda bi: (layer, bi, 0, 0))],
        out_specs=pl.BlockSpec((1, tq, lpad), lambda bi: (bi, 0, 0)),
        out_shape=jax.ShapeDtypeStruct((b, tq, lpad), BF16),
        scratch_shapes=[pltpu.VMEM((tq, lpad), jnp.int32), pltpu.VMEM((tq, lpad), jnp.int16),
                        pltpu.VMEM((tq, lpad), jnp.int16)],
        compiler_params=_cparams(("arbitrary",)),
        name="select_full",
    )(proj, proj, kidx_cache)


CONVB_ROWS = 64
CONVB_HEAD = 32


def _convb_kernel(a_ref, gt_ref, init_ref, w_ref, dwb_ref, lng_ref, lnb_ref, o_ref, tail_ref, f_ref):
    tt = a_ref.shape[1]
    lead = CONVB_HEAD - (B_WIDTH - 1)

    @pl.when(pl.program_id(1) == 0)
    def _():
        f_ref[0:CONVB_HEAD, :] = init_ref[0]

    f_ref[CONVB_HEAD:CONVB_HEAD + tt, :] = a_ref[0] * jax.nn.sigmoid(gt_ref[0])
    for r in range(tt // CONVB_ROWS):
        acc = None
        for k in range(B_WIDTH):
            s0 = r * CONVB_ROWS + lead + k
            term = w_ref[k:k + 1, :] * f_ref[s0:s0 + CONVB_ROWS, :]
            acc = term if acc is None else acc + term
        y = acc + dwb_ref[...]
        yc = y - jnp.mean(y, axis=-1, keepdims=True)
        yn = yc * lax.rsqrt(jnp.mean(yc * yc, axis=-1, keepdims=True) + EPS) * lng_ref[...] + lnb_ref[...]
        o_ref[0, r * CONVB_ROWS:(r + 1) * CONVB_ROWS, :] = _silu(yn)
    last = f_ref[tt:tt + CONVB_HEAD, :]
    tail_ref[0] = last
    f_ref[0:CONVB_HEAD, :] = last


def conv_module(proj, init, w, dwb, lng, lnb, tt):
    b, t, _ = proj.shape
    cb = COL_BGLU // B_CH
    vec = pl.BlockSpec((1, B_CH), lambda bi, j: (0, 0))
    return pl.pallas_call(
        _convb_kernel,
        grid=(b, t // tt),
        in_specs=[pl.BlockSpec((1, tt, B_CH), lambda bi, j: (bi, j, cb)),
                  pl.BlockSpec((1, tt, B_CH), lambda bi, j: (bi, j, cb + 1)),
                  pl.BlockSpec((1, CONVB_HEAD, B_CH), lambda bi, j: (bi, 0, 0)),
                  pl.BlockSpec((B_WIDTH, B_CH), lambda bi, j: (0, 0)),
                  vec, vec, vec],
        out_specs=[pl.BlockSpec((1, tt, B_CH), lambda bi, j: (bi, j, 0)),
                   pl.BlockSpec((1, CONVB_HEAD, B_CH), lambda bi, j: (bi, 0, 0))],
        out_shape=[jax.ShapeDtypeStruct((b, t, B_CH), F32), jax.ShapeDtypeStruct((b, CONVB_HEAD, B_CH), F32)],
        scratch_shapes=[pltpu.VMEM((CONVB_HEAD + tt, B_CH), F32)],
        compiler_params=_cparams(("arbitrary", "arbitrary")),
        name="conv_module",
    )(proj, proj, init, w, dwb, lng, lnb)


GDN_HEAD = 8
GDN_BB = 4
INV_BASE = 8


def _unit_lower_inverses(mats):
    n = mats[0].shape[0]
    r = lax.broadcasted_iota(jnp.int32, (n, n), 0)
    c = lax.broadcasted_iota(jnp.int32, (n, n), 1)
    eye = jnp.where(r == c, 1.0, 0.0)

    def same_block(size):
        shift = int(math.log2(size))
        return jnp.right_shift(r, shift) == jnp.right_shift(c, shift)

    diag = same_block(INV_BASE)
    xs = [jnp.where(diag, -a, 0.0) for a in mats]
    ps = [eye + x for x in xs]
    splits = [_split_bf16(x) for x in xs]
    for _ in range(int(math.log2(INV_BASE)) - 1):
        splits = [_split_bf16(_dot_split(s, s)) for s in splits]
        ps = [p + _dot_split(_split_bf16(p), s) for p, s in zip(ps, splits)]
    size = INV_BASE
    while size < n:
        off = same_block(2 * size) & jnp.logical_not(same_block(size))
        lows = [_split_bf16(jnp.where(off, a, 0.0)) for a in mats]
        psplit = [_split_bf16(p) for p in ps]
        mids = [_split_bf16(_dot_split(lo, p)) for lo, p in zip(lows, psplit)]
        ps = [p - _dot_split(ph, mid) for p, ph, mid in zip(ps, psplit, mids)]
        size *= 2
    return ps


def _split_bf16(x):
    hi = x.astype(BF16)
    return hi, (x - hi.astype(F32)).astype(BF16)


def _dot_split(a, b):
    (ah, al), (bh, bl) = a, b
    return _dot(ah, bh) + (_dot(ah, bl) + _dot(al, bh))


def _gdn_kernel(x_ref, z_ref, misc_ref, cinit_ref, s0_ref, cw_ref, alog_ref, dtb_ref, ng_ref,
                o_ref, tail_ref, sout_ref, f_ref, s_ref):
    c = pl.program_id(1)

    @pl.when(c == 0)
    def _():
        f_ref[:, 0:GDN_HEAD, :] = cinit_ref[...]
        s_ref[...] = s0_ref[...]

    bb, cc = x_ref.shape[0], x_ref.shape[1]
    lead = GDN_HEAD - (D_CONV - 1)
    r = lax.broadcasted_iota(jnp.int32, (cc, cc), 0)
    col = lax.broadcasted_iota(jnp.int32, (cc, cc), 1)
    incl = col <= r
    strict = col < r
    lower = jnp.where(incl, 1.0, 0.0)

    ys, betas, gcums, gcum_ts = [], [], [], []
    for bi in range(bb):
        u = x_ref[bi]
        f_ref[bi, GDN_HEAD:GDN_HEAD + cc, :] = u
        y = None
        for j in range(D_CONV):
            term = cw_ref[j:j + 1, :] * f_ref[bi, lead + j:lead + j + cc, :]
            y = term if y is None else y + term
        ys.append(_silu(y))
        last = u[cc - GDN_HEAD:cc, :]
        tail_ref[bi] = last
        f_ref[bi, 0:GDN_HEAD, :] = last
        misc = misc_ref[bi]
        betas.append(jax.nn.sigmoid(misc))
        xg = misc + dtb_ref[...]
        softplus = jnp.maximum(xg, 0.0) + jnp.log(1.0 + jnp.exp(-jnp.abs(xg)))
        g_all = -jnp.exp(alog_ref[...]) * softplus
        gcum = _dot(lower, g_all, HIGHEST)
        gcums.append(gcum)
        gcum_ts.append(jnp.transpose(gcum))

    prob = [(bi, h) for bi in range(bb) for h in range(D_HEADS)]
    qn, kn, vh, bc, gc, gl, decay = [], [], [], [], [], [], []
    for bi, h in prob:
        y = ys[bi]
        q = y[:, h * D_KDIM:(h + 1) * D_KDIM]
        k = y[:, D_QK + h * D_KDIM:D_QK + (h + 1) * D_KDIM]
        vh.append(y[:, 2 * D_QK + h * D_VDIM:2 * D_QK + (h + 1) * D_VDIM])
        qn.append(q * lax.rsqrt(jnp.sum(q * q, axis=-1, keepdims=True) + EPS) * (D_KDIM ** -0.5))
        kn.append(k * lax.rsqrt(jnp.sum(k * k, axis=-1, keepdims=True) + EPS))
        bc.append(betas[bi][:, MISC_DB + h:MISC_DB + h + 1])
        g_col = gcums[bi][:, MISC_DA + h:MISC_DA + h + 1]
        g_row = gcum_ts[bi][MISC_DA + h:MISC_DA + h + 1, :]
        gc.append(g_col)
        gl.append(gcums[bi][cc - 1:cc, MISC_DA + h:MISC_DA + h + 1])
        decay.append(jnp.where(incl, jnp.exp(jnp.where(incl, g_col - g_row, 0.0)), 0.0))
    n = len(prob)
    kb = [x.astype(BF16) for x in kn]
    qb = [x.astype(BF16) for x in qn]
    kk = [_dot_nt(kb[g], kb[g]) for g in range(n)]
    tinv = _unit_lower_inverses([jnp.where(strict, bc[g] * decay[g] * kk[g], 0.0) for g in range(n)])
    s = [s_ref[bi, h] for bi, h in prob]
    sb = [x.astype(BF16) for x in s]
    eg = [jnp.exp(x) for x in gc]
    ks = [_dot(kb[g], sb[g]) for g in range(n)]
    uu = [_dot_split(_split_bf16(tinv[g]), _split_bf16(bc[g] * (vh[g] - eg[g] * ks[g]))) for g in range(n)]
    ub = [x.astype(BF16) for x in uu]
    qk = [(_dot_nt(qb[g], kb[g]) * decay[g]).astype(BF16) for g in range(n)]
    qs = [_dot(qb[g], sb[g]) for g in range(n)]
    o = [eg[g] * qs[g] + _dot(qk[g], ub[g]) for g in range(n)]
    kd = [(kn[g] * jnp.exp(gl[g] - gc[g])).astype(BF16) for g in range(n)]
    s_new = [jnp.exp(gl[g]) * s[g] + _dot_tn(kd[g], ub[g]) for g in range(n)]
    for g, (bi, h) in enumerate(prob):
        s_ref[bi, h] = s_new[g]
        zh = z_ref[bi, :, h * D_VDIM:(h + 1) * D_VDIM]
        o_ref[bi, :, h * D_VDIM:(h + 1) * D_VDIM] = _rms(o[g], ng_ref[...]) * _silu(zh)

    @pl.when(c == pl.num_programs(1) - 1)
    def _():
        sout_ref[...] = s_ref[...]


def gated_delta(proj, cinit, s0, cw, alog, dtb, ng):
    b, t, _ = proj.shape
    cc = min(t, CHUNK)
    bb = math.gcd(b, GDN_BB)
    vec = pl.BlockSpec((1, LANES), lambda bi, c: (0, 0))
    return pl.pallas_call(
        _gdn_kernel,
        grid=(b // bb, t // cc),
        in_specs=[pl.BlockSpec((bb, cc, D_CONV_CH), lambda bi, c: (bi, c, COL_DQKV // D_CONV_CH)),
                  pl.BlockSpec((bb, cc, D_V), lambda bi, c: (bi, c, COL_DZ // D_V)),
                  pl.BlockSpec((bb, cc, LANES), lambda bi, c: (bi, c, COL_MISC // LANES)),
                  pl.BlockSpec((bb, GDN_HEAD, D_CONV_CH), lambda bi, c: (bi, 0, 0)),
                  pl.BlockSpec((bb, D_HEADS, D_KDIM, D_VDIM), lambda bi, c: (bi, 0, 0, 0)),
                  pl.BlockSpec((D_CONV, D_CONV_CH), lambda bi, c: (0, 0)),
                  vec, vec, vec],
        out_specs=[pl.BlockSpec((bb, cc, D_V), lambda bi, c: (bi, c, 0)),
                   pl.BlockSpec((bb, GDN_HEAD, D_CONV_CH), lambda bi, c: (bi, 0, 0)),
                   pl.BlockSpec((bb, D_HEADS, D_KDIM, D_VDIM), lambda bi, c: (bi, 0, 0, 0))],
        out_shape=[jax.ShapeDtypeStruct((b, t, D_V), F32),
                   jax.ShapeDtypeStruct((b, GDN_HEAD, D_CONV_CH), F32),
                   jax.ShapeDtypeStruct((b, D_HEADS, D_KDIM, D_VDIM), F32)],
        scratch_shapes=[pltpu.VMEM((bb, GDN_HEAD + cc, D_CONV_CH), F32),
                        pltpu.VMEM((bb, D_HEADS, D_KDIM, D_VDIM), F32)],
        compiler_params=_cparams(("arbitrary", "arbitrary")),
        name="gated_delta",
    )(proj, proj, proj, cinit, s0, cw, alog, dtb, ng)


def _merge_kernel(oa_ref, ob_ref, oc_ref, od_ref, g0_ref, g1_ref, g2_ref, g3_ref, x_ref, gt_ref, ng_ref,
                  wbr_ref, wout_ref, o_ref):
    bt, tt, d = x_ref.shape
    rows = bt * tt
    merged = None
    for m, (br, gate) in enumerate(((oa_ref, g0_ref), (ob_ref, g1_ref), (oc_ref, g2_ref), (od_ref, g3_ref))):
        term = jax.nn.sigmoid(gate[...].reshape(rows, d)) * _dot(br[...].reshape(rows, BRANCH_W).astype(BF16), wbr_ref[m])
        merged = term if merged is None else merged + term
    mix = _dot(merged.astype(BF16), wout_ref[...])
    o_ref[...] = x_ref[...] + gt_ref[...] * _rms(mix, ng_ref[...]).reshape(bt, tt, d)


def merge_out(branches, proj, x, mod, ng, wbr, wout, bt, tt):
    b, t, d = x.shape
    gb = COL_GATE // d
    br_spec = pl.BlockSpec((bt, tt, BRANCH_W), lambda i, j: (i, j, 0))
    gate_specs = [pl.BlockSpec((bt, tt, d), functools.partial(lambda i, j, m: (i, j, gb + m), m=m))
                  for m in range(N_BRANCH)]
    return pl.pallas_call(
        _merge_kernel,
        grid=(b // bt, t // tt),
        in_specs=[br_spec] * 4 + gate_specs + [
            pl.BlockSpec((bt, tt, d), lambda i, j: (i, j, 0)),
            pl.BlockSpec((bt, 1, d), lambda i, j: (i, 0, 2)),
            pl.BlockSpec((1, d), lambda i, j: (0, 0)),
            pl.BlockSpec((N_BRANCH, BRANCH_W, d), lambda i, j: (0, 0, 0)),
            pl.BlockSpec((d, d), lambda i, j: (0, 0))],
        out_specs=pl.BlockSpec((bt, tt, d), lambda i, j: (i, j, 0)),
        out_shape=jax.ShapeDtypeStruct((b, t, d), F32),
        compiler_params=_cparams(("arbitrary", "arbitrary")),
        name="merge_out",
    )(*branches, proj, proj, proj, proj, x, mod, ng, wbr, wout)


def _mlp_kernel(x_ref, sh_ref, sc_ref, gt_ref, g2_ref, g3_ref, w1_ref, w2_ref, o_ref, h_ref, acc_ref):
    bt, tt, d = x_ref.shape
    f = pl.program_id(2)

    @pl.when(f == 0)
    def _():
        h = _rms(x_ref[...], g2_ref[...]) * (1.0 + sc_ref[...]) + sh_ref[...]
        h_ref[...] = h.reshape(bt * tt, d).astype(BF16)
        acc_ref[...] = jnp.zeros(acc_ref.shape, F32)

    a = jnp.maximum(_dot(h_ref[...], w1_ref[...]), 0.0)
    acc_ref[...] += _dot((a * a).astype(BF16), w2_ref[...])

    @pl.when(f == pl.num_programs(2) - 1)
    def _():
        o_ref[...] = x_ref[...] + gt_ref[...] * _rms(acc_ref[...], g3_ref[...]).reshape(bt, tt, d)


def mlp(x, mod, g2, g3, w1, w2, bt, tt):
    b, t, d = x.shape
    ff = w1.shape[1]
    tf = 1024
    return pl.pallas_call(
        _mlp_kernel,
        grid=(b // bt, t // tt, ff // tf),
        in_specs=[pl.BlockSpec((bt, tt, d), lambda i, j, f: (i, j, 0)),
                  pl.BlockSpec((bt, 1, d), lambda i, j, f: (i, 0, 3)),
                  pl.BlockSpec((bt, 1, d), lambda i, j, f: (i, 0, 4)),
                  pl.BlockSpec((bt, 1, d), lambda i, j, f: (i, 0, 5)),
                  pl.BlockSpec((1, d), lambda i, j, f: (0, 0)),
                  pl.BlockSpec((1, d), lambda i, j, f: (0, 0)),
                  pl.BlockSpec((d, tf), lambda i, j, f: (0, f)),
                  pl.BlockSpec((tf, d), lambda i, j, f: (f, 0))],
        out_specs=pl.BlockSpec((bt, tt, d), lambda i, j, f: (i, j, 0)),
        out_shape=jax.ShapeDtypeStruct((b, t, d), F32),
        scratch_shapes=[pltpu.VMEM((bt * tt, d), BF16), pltpu.VMEM((bt * tt, d), F32)],
        compiler_params=_cparams(("arbitrary", "arbitrary", "arbitrary")),
        name="mlp",
    )(x, mod, mod, mod, g2, g3, w1, w2)


def _combined_in_weight(w_in_l, w_gate_l):
    offs = np.concatenate([[0], np.cumsum(IN_SIZES)])
    (aq, ak, av, bglu, cq, ck, cv, iq, ik, iw, dqkv, dz, db, da) = [
        w_in_l[:, int(offs[i]):int(offs[i + 1])] for i in range(len(IN_SIZES))]
    d = w_in_l.shape[0]
    misc = jnp.concatenate([ik, iw, db, da, jnp.zeros((d, LANES - IDX_DIM - IDX_HEADS - 2 * D_HEADS), F32)], axis=1)
    pad = jnp.zeros((d, COL_GATE - COL_MISC - LANES), F32)
    gates = [w_gate_l[m] for m in range(N_BRANCH)]
    w = jnp.concatenate([aq, ak, av, bglu, cq, ck, cv, dqkv, dz, iq, misc, pad] + gates, axis=1)
    assert w.shape[1] == N_PROJ
    return w.astype(BF16)


def _lane_pad(v, offset):
    return jnp.zeros((1, LANES), F32).at[0, offset:offset + v.shape[0]].set(v)


def _run_group(x, mod, lw, past, bias, rel_bias, tiles):
    b, t, d = x.shape
    bt, tt, conv_tt = tiles
    proj = inproj(x, mod, lw['g'][0:1], lw['w_all'], bt, tt)

    if past is None:
        b_init = jnp.zeros((b, CONVB_HEAD, B_CH), F32)
        d_init = jnp.zeros((b, GDN_HEAD, D_CONV_CH), F32)
        s0 = jnp.zeros((b, D_HEADS, D_KDIM, D_VDIM), F32)
        o_a = flash_causal(proj, bias, rel_bias, lw['lam_init'], diff=True, qcol=COL_AQ, kcol=COL_AK, vcol=COL_AV,
                           bias_col0=0, a_lambda=lw['a_lambda'], a_norm_g=lw['a_norm_g'])
        sel = select_causal(proj)
        o_c = flash_causal(proj, bias, rel_bias, lw['lam_init'], diff=False, qcol=COL_CQ, kcol=COL_CK, vcol=COL_CV,
                           bias_col0=A_HEADS, mask=sel)
    else:
        layer = past['layer']
        lead_b = CONVB_HEAD - (B_WIDTH - 1)
        b_init = jnp.pad(past['b_conv'][layer], ((0, 0), (lead_b, 0), (0, 0)))
        d_init = jnp.pad(past['d_conv'][layer], ((0, 0), (GDN_HEAD - (D_CONV - 1), 0), (0, 0)))
        s0 = past['d_state'][layer]
        o_a = flash_full(proj, past['a_k'], past['a_v'], layer, bias, lw['lam_init'], diff=True,
                         qcol=COL_AQ, kcol=COL_AK, vcol=COL_AV, bias_col0=0,
                         a_lambda=lw['a_lambda'], a_norm_g=lw['a_norm_g'])
        sel = select_full(proj, past['c_kidx'], layer)
        o_c = flash_full(proj, past['c_k'], past['c_v'], layer, bias, lw['lam_init'], diff=False,
                         qcol=COL_CQ, kcol=COL_CK, vcol=COL_CV, bias_col0=A_HEADS, mask=sel)

    o_b, b_tail = conv_module(proj, b_init, lw['b_dw_w'], lw['b_dw_b'], lw['b_ln_g'], lw['b_ln_b'], conv_tt)
    o_d, d_tail, s_new = gated_delta(proj, d_init, s0, lw['d_conv_w'], lw['alog'], lw['dtb'], lw['d_norm_g'])

    mbt, mtt = (1, 256) if t >= 256 else (min(b, 256 // t), t)
    x1 = merge_out((o_a, o_b, o_c, o_d), proj, x, mod, lw['g'][1:2], lw['w_br'], lw['w_out'], mbt, mtt)
    fbt, ftt = (1, 512) if t >= 512 else (min(b, 512 // t), t)
    x2 = mlp(x1, mod, lw['g'][2:3], lw['g'][3:4], lw['w1'], lw['w2'], fbt, ftt)

    new = {
        'a_k': proj[:, :, COL_AK:COL_AK + A_QK].reshape(b, t, A_HEADS, 2 * A_DIM),
        'a_v': proj[:, :, COL_AV:COL_AV + A_QK].reshape(b, t, A_HEADS, 2 * A_DIM),
        'c_k': proj[:, :, COL_CK:COL_CK + C_W].reshape(b, t, C_HEADS, C_DIM),
        'c_v': proj[:, :, COL_CV:COL_CV + C_W].reshape(b, t, C_HEADS, C_DIM),
        'c_kidx': proj[:, :, COL_MISC:COL_MISC + IDX_DIM],
        'b_conv': b_tail[:, CONVB_HEAD - (B_WIDTH - 1):, :],
        'd_conv': d_tail[:, GDN_HEAD - (D_CONV - 1):, :],
        'd_state': s_new,
    }
    return x2, new


def kernel(x_prompt, x_sample, c_prompt, c_sample, cache_a_k, cache_a_v, cache_c_k, cache_c_v, cache_c_kidx, state_b_conv, state_d_conv, state_d_state, rel_bias, ada_w, ada_b, norm_g, w_in, a_lambda, a_norm_g, b_dw_w, b_dw_b, b_ln_g, b_ln_b, d_conv_w, d_a_log, d_dt_bias, d_norm_g, w_gate, w_br, w_out, mlp_w1, mlp_w2):
    depth = w_in.shape[0]
    bp, tp, d = x_prompt.shape
    bs, ts, _ = x_sample.shape
    past_len = cache_a_k.shape[2]

    mod_all = adaln_mod(jnp.concatenate([c_prompt, c_sample], axis=0), ada_w, ada_b)
    bias_p = bias_tiles(rel_bias, FLASH_T, 2 * FLASH_T, FLASH_T, 0)
    bias_s = bias_tiles(rel_bias, ts, past_len + ts, past_len, 0)

    past = {
        'a_k': cache_a_k.reshape(depth, bs, past_len, A_QK),
        'a_v': cache_a_v.reshape(depth, bs, past_len, A_QK),
        'c_k': cache_c_k.reshape(depth, bs, past_len, C_W),
        'c_v': cache_c_v.reshape(depth, bs, past_len, C_W),
        'c_kidx': cache_c_kidx, 'b_conv': state_b_conv, 'd_conv': state_d_conv, 'd_state': state_d_state,
    }
    names = ('a_k', 'a_v', 'c_k', 'c_v', 'c_kidx', 'b_conv', 'd_conv', 'd_state')
    new_p = {n: [] for n in names}
    new_s = {n: [] for n in names}
    xp, xs = x_prompt, x_sample
    for l in range(depth):
        lw = {
            'g': norm_g[l],
            'w_all': _combined_in_weight(w_in[l], w_gate[l]),
            'lam_init': jnp.full((1,), 0.8 - 0.6 * math.exp(-0.3 * l), F32),
            'a_lambda': a_lambda[l],
            'a_norm_g': a_norm_g[l].reshape(1, LANES),
            'b_dw_w': b_dw_w[l], 'b_dw_b': b_dw_b[l].reshape(1, B_CH),
            'b_ln_g': b_ln_g[l].reshape(1, B_CH), 'b_ln_b': b_ln_b[l].reshape(1, B_CH),
            'd_conv_w': d_conv_w[l],
            'alog': _lane_pad(d_a_log[l], MISC_DA), 'dtb': _lane_pad(d_dt_bias[l], MISC_DA),
            'd_norm_g': d_norm_g[l].reshape(1, LANES),
            'w_br': w_br[l].astype(BF16), 'w_out': w_out[l].astype(BF16),
            'w1': mlp_w1[l].astype(BF16), 'w2': mlp_w2[l].astype(BF16),
        }
        mod_p = mod_all[l, :bp].reshape(bp, 1, 6 * d)
        mod_s = mod_all[l, bp:].reshape(bs, 1, 6 * d)
        xp, sp = _run_group(xp, mod_p, lw, None, bias_p, rel_bias, (1, 1024, 256))
        xs, ss = _run_group(xs, mod_s, lw, dict(past, layer=l), bias_s, rel_bias, (min(bs, 1024 // ts), ts, ts))
        for n in names:
            new_p[n].append(sp[n])
            new_s[n].append(ss[n])
    return (xp, xs) + tuple(jnp.stack(new_p[n]) for n in names) + tuple(jnp.stack(new_s[n]) for n in names)
```

```python
import functools
import math

import numpy as np
import jax
import jax.numpy as jnp
from jax import lax
from jax.experimental import pallas as pl
from jax.experimental.pallas import tpu as pltpu

F32 = jnp.float32
BF16 = jnp.bfloat16
HIGHEST = lax.Precision.HIGHEST

D_MODEL = 1024
CHUNK = 64
BRANCH_W = D_MODEL // 2
N_BRANCH = 4
A_HEADS = 4
A_DIM = 64
B_CH = BRANCH_W
B_WIDTH = 31
C_HEADS = 8
C_DIM = 64
IDX_HEADS = 4
IDX_DIM = 64
TOPK_MAX = 256
D_HEADS = 4
D_VDIM = 128
D_KDIM = 64
D_CONV = 4
NUM_BUCKETS = 32
MAX_DISTANCE = 128
D_FF = 4 * D_MODEL
EPS = 1e-6
A_QK = A_HEADS * 2 * A_DIM
C_W = C_HEADS * C_DIM
D_QK = D_HEADS * D_KDIM
D_V = D_HEADS * D_VDIM
D_CONV_CH = 2 * D_QK + D_V
IN_SIZES = (A_QK, A_QK, A_QK, 2 * B_CH, C_W, C_W, C_W, IDX_HEADS * IDX_DIM, IDX_DIM, IDX_HEADS,
            D_CONV_CH, D_V, D_HEADS, D_HEADS)

LANES = 128
SUBLANES = 8
NEG = -1e30
LOG2E = 1.4426950408889634
INT_MIN = -2 ** 31
VMEM_LIMIT = 56 * 1024 * 1024

COL_AQ, COL_AK, COL_AV = 0, 512, 1024
COL_BGLU = 1536
COL_CQ, COL_CK, COL_CV = 2560, 3072, 3584
COL_DQKV = 4096
COL_DZ = 5120
COL_IQ = 5632
COL_MISC = 5888
COL_GATE = 6144
N_PROJ = 10240
MISC_IW, MISC_DB, MISC_DA = 64, 68, 72

FLASH_T = 256
FLASH_FAR = 1024
FLASH_PAIRS = 2
SEL_TQ = 512
SEL_KB = 256
SEL_ROWS = 512
SEL_PRE = 256
I16_MIN = -2 ** 15


def _cparams(sem):
    return pltpu.CompilerParams(dimension_semantics=sem, vmem_limit_bytes=VMEM_LIMIT)


def _dot(a, b, precision=None):
    return jnp.dot(a, b, preferred_element_type=F32, precision=precision)


def _dot_nt(a, b):
    return lax.dot_general(a, b, (((1,), (1,)), ((), ())), preferred_element_type=F32)


def _dot_tn(a, b):
    return lax.dot_general(a, b, (((0,), (0,)), ((), ())), preferred_element_type=F32)


def _rms(x, g):
    return x * lax.rsqrt(jnp.mean(x * x, axis=-1, keepdims=True) + EPS) * g


def _silu(x):
    return x * jax.nn.sigmoid(x)


def _chunk_of(pos):
    return jnp.right_shift(pos, int(math.log2(CHUNK)))


def _mod_kernel(c_ref, w_ref, b_ref, o_ref):
    s = _silu(c_ref[...])
    o_ref[0] = _dot(s.astype(BF16), w_ref[0].astype(BF16)) + b_ref[0]


def adaln_mod(c_all, ada_w, ada_b):
    depth, d, n = ada_w.shape
    bc = c_all.shape[0]
    tn = 1024
    return pl.pallas_call(
        _mod_kernel,
        grid=(depth, n // tn),
        in_specs=[pl.BlockSpec((bc, d), lambda l, j: (0, 0)),
                  pl.BlockSpec((1, d, tn), lambda l, j: (l, 0, j)),
                  pl.BlockSpec((1, 1, tn), lambda l, j: (l, 0, j))],
        out_specs=pl.BlockSpec((1, bc, tn), lambda l, j: (l, 0, j)),
        out_shape=jax.ShapeDtypeStruct((depth, bc, n), F32),
        compiler_params=_cparams(("arbitrary", "arbitrary")),
        name="adaln_mod",
    )(c_all, ada_w, ada_b.reshape(depth, 1, n))


def _bias_kernel(tab_ref, o_ref, *, q0, k0):
    h = pl.program_id(0)
    tq, w = o_ref.shape[1], o_ref.shape[2]
    row = lax.broadcasted_iota(jnp.int32, (tq, w), 0)
    col = lax.broadcasted_iota(jnp.int32, (tq, w), 1)
    rel = (col + k0) - (row + q0)
    nb = NUM_BUCKETS // 2
    max_exact = nb // 2
    n = jnp.abs(rel)
    large = max_exact + (jnp.log(jnp.maximum(n, max_exact).astype(F32) / max_exact)
                         / math.log(MAX_DISTANCE / max_exact) * (nb - max_exact)).astype(jnp.int32)
    large = jnp.minimum(large, nb - 1)
    bucket = jnp.where(rel > 0, nb, 0) + jnp.where(n < max_exact, n, large)
    val = jnp.zeros((tq, w), F32)
    for bk in range(NUM_BUCKETS):
        val = jnp.where(bucket == bk, tab_ref[bk, h], val)
    o_ref[0] = val * LOG2E


def bias_tiles(rel_bias, tq, w, q0, k0):
    nh = rel_bias.shape[1]
    return pl.pallas_call(
        functools.partial(_bias_kernel, q0=q0, k0=k0),
        grid=(nh,),
        in_specs=[pl.BlockSpec(memory_space=pltpu.SMEM)],
        out_specs=pl.BlockSpec((1, tq, w), lambda h: (h, 0, 0)),
        out_shape=jax.ShapeDtypeStruct((nh, tq, w), F32),
        compiler_params=_cparams(("arbitrary",)),
        name="bias_tiles",
    )(rel_bias)


def _inproj_kernel(x_ref, sh_ref, sc_ref, g_ref, w_ref, o_ref, h_ref):
    bt, tt, d = x_ref.shape

    @pl.when(pl.program_id(2) == 0)
    def _():
        h = _rms(x_ref[...], g_ref[...]) * (1.0 + sc_ref[...]) + sh_ref[...]
        h_ref[...] = h.reshape(bt * tt, d).astype(BF16)

    o_ref[...] = _dot(h_ref[...], w_ref[...]).reshape(o_ref.shape)


def inproj(x, mod, g, w_all, bt, tt):
    b, t, d = x.shape
    n = w_all.shape[1]
    tn = 2048
    return pl.pallas_call(
        _inproj_kernel,
        grid=(b // bt, t // tt, n // tn),
        in_specs=[pl.BlockSpec((bt, tt, d), lambda i, j, k: (i, j, 0)),
                  pl.BlockSpec((bt, 1, d), lambda i, j, k: (i, 0, 0)),
                  pl.BlockSpec((bt, 1, d), lambda i, j, k: (i, 0, 1)),
                  pl.BlockSpec((1, d), lambda i, j, k: (0, 0)),
                  pl.BlockSpec((d, tn), lambda i, j, k: (0, k))],
        out_specs=pl.BlockSpec((bt, tt, tn), lambda i, j, k: (i, j, k)),
        out_shape=jax.ShapeDtypeStruct((b, t, n), F32),
        scratch_shapes=[pltpu.VMEM((bt * tt, d), BF16)],
        compiler_params=_cparams(("arbitrary", "arbitrary", "arbitrary")),
        name="inproj",
    )(x, mod, mod, g, w_all)


def _flash_tile(qs, kbs, vbs, biases, shifts, add_mask, keep, m_ref, l_ref, acc_ref):
    scores = [_dot_nt(q, kbs[m // 2]) for m, q in enumerate(qs)]
    for m, s in enumerate(scores):
        if biases[m] is not None:
            s = s + biases[m]
        if add_mask is not None:
            s = s + add_mask
        if keep is not None:
            s = jnp.where(keep, s, NEG)
        m_prev = m_ref[m]
        s_max = jnp.max(s, axis=-1, keepdims=True)
        if shifts[m] is not None:
            s_max = s_max + shifts[m]
        m_new = jnp.maximum(m_prev, s_max)
        alpha = jnp.exp2(m_prev - m_new)
        p = jnp.exp2(s - (m_new if shifts[m] is None else m_new - shifts[m]))
        l_ref[m] = alpha * l_ref[m] + jnp.sum(p, axis=-1, keepdims=True)
        acc_ref[m] = alpha * acc_ref[m] + _dot(p.astype(BF16), vbs[m // 2])
        m_ref[m] = m_new


def _flash_finish(diff, lam_ref, alam_ref, g_ref, o_ref, l_ref, acc_ref, lane):
    for j in range(o_ref.shape[2] // LANES):
        o0 = acc_ref[2 * j] / l_ref[2 * j]
        o1 = acc_ref[2 * j + 1] / l_ref[2 * j + 1]
        if diff:
            lv = alam_ref[...]
            lam_init = lam_ref[0]
            lam = (jnp.exp(jnp.sum(lv[0:1] * lv[1:2], axis=-1, keepdims=True))
                   - jnp.exp(jnp.sum(lv[2:3] * lv[3:4], axis=-1, keepdims=True)) + lam_init)
            o = o0 - lam * o1
            o = _rms(o, g_ref[...]) * (1.0 - lam_init)
        else:
            o = jnp.where(lane < C_DIM, o0, o1)
        o_ref[0, :, j * LANES:(j + 1) * LANES] = o


def _flash_init(q_ref, m_ref, l_ref, acc_ref):
    lane = lax.broadcasted_iota(jnp.int32, (1, LANES), 1)
    qs = []
    for j in range(q_ref.shape[2] // LANES):
        q = q_ref[0, :, j * LANES:(j + 1) * LANES] * (A_DIM ** -0.5 * LOG2E)
        qs.append(jnp.where(lane < A_DIM, q, 0.0).astype(BF16))
        qs.append(jnp.where(lane >= A_DIM, q, 0.0).astype(BF16))
    m_ref[...] = jnp.full(m_ref.shape, NEG, F32)
    l_ref[...] = jnp.zeros(l_ref.shape, F32)
    acc_ref[...] = jnp.zeros(acc_ref.shape, F32)
    return lane, qs


def _flash_causal_kernel(*refs, diff, has_mask, col0):
    tab_ref, lam_ref, q_ref, k_ref, v_ref, bias_ref = refs[:6]
    pos = 6
    mask_ref = None
    if has_mask:
        mask_ref = refs[pos]
        pos += 1
    alam_ref = g_ref = None
    if diff:
        alam_ref, g_ref = refs[pos], refs[pos + 1]
        pos += 2
    o_ref, m_ref, l_ref, acc_ref = refs[pos:pos + 4]
    t = q_ref.shape[1]
    npair = q_ref.shape[2] // LANES
    nb = bias_ref.shape[0] // npair
    p = pl.program_id(1)
    i = pl.program_id(2)
    lane, qs = _flash_init(q_ref, m_ref, l_ref, acc_ref)
    bias_of = [nb * (m // 2) + min(m % 2, nb - 1) for m in range(2 * npair)]
    none = (None,) * (2 * npair)
    far = [tab_ref[NUM_BUCKETS // 2 - 1, col0 + nb * npair * p + bias_of[m]] * LOG2E for m in range(2 * npair)]

    def tile(ks, width, biases, shifts, keep):
        kbs = [k_ref[0, pl.ds(ks, width), j * LANES:(j + 1) * LANES].astype(BF16) for j in range(npair)]
        vbs = [v_ref[0, pl.ds(ks, width), j * LANES:(j + 1) * LANES].astype(BF16) for j in range(npair)]
        add = mask_ref[0, :, pl.ds(ks, width)].astype(F32) if has_mask else None
        _flash_tile(qs, kbs, vbs, biases, shifts, add, keep, m_ref, l_ref, acc_ref)

    n_far = jnp.maximum(i - 1, 0)
    wide = FLASH_FAR // t
    n_wide = n_far // wide

    def wide_body(kj, carry):
        tile(pl.multiple_of(kj * FLASH_FAR, FLASH_FAR), FLASH_FAR, none, far, None)
        return carry

    lax.fori_loop(0, n_wide, wide_body, 0)
    width = FLASH_FAR // 2
    done = n_wide * wide
    while width >= t:
        has = ((n_far - done) * t) >= width
        start = pl.multiple_of(done * t, t)

        @pl.when(has)
        def _(start=start, width=width):
            tile(start, width, none, far, None)

        done = done + jnp.where(has, width // t, 0)
        width //= 2

    row = lax.broadcasted_iota(jnp.int32, (t, 2 * t), 0)
    col = lax.broadcasted_iota(jnp.int32, (t, 2 * t), 1)
    keep = _chunk_of(col) <= _chunk_of(row) + t // CHUNK

    @pl.when(i == 0)
    def _():
        tile(0, t, [bias_ref[k, :, t:2 * t] for k in bias_of], none, keep[:, t:2 * t])

    @pl.when(i >= 1)
    def _():
        tile(pl.multiple_of((i - 1) * t, t), 2 * t, [bias_ref[k] for k in bias_of], none, keep)

    _flash_finish(diff, lam_ref, alam_ref, g_ref, o_ref, l_ref, acc_ref, lane)


def flash_causal(proj, near_bias, rel_bias, lam_init, *, diff, qcol, kcol, vcol, bias_col0,
                 mask=None, a_lambda=None, a_norm_g=None):
    b, t_all, _ = proj.shape
    t = FLASH_T
    npair = 4
    pp = FLASH_PAIRS
    w = pp * LANES
    nb = (1 if diff else 2) * pp
    qb, kb, vb = qcol // w, kcol // w, vcol // w
    bb0 = bias_col0 // nb
    in_specs = [pl.BlockSpec(memory_space=pltpu.SMEM),
                pl.BlockSpec(memory_space=pltpu.SMEM),
                pl.BlockSpec((1, t, w), lambda bi, p, i: (bi, i, qb + p)),
                pl.BlockSpec((1, t_all, w), lambda bi, p, i: (bi, 0, kb + p)),
                pl.BlockSpec((1, t_all, w), lambda bi, p, i: (bi, 0, vb + p)),
                pl.BlockSpec((nb, t, 2 * t), lambda bi, p, i: (bb0 + p, 0, 0))]
    args = [rel_bias, lam_init, proj, proj, proj, near_bias]
    if mask is not None:
        in_specs.append(pl.BlockSpec((1, t, t_all), lambda bi, p, i: (bi, i, 0)))
        args.append(mask)
    if diff:
        in_specs += [pl.BlockSpec((4, A_DIM), lambda bi, p, i: (0, 0)),
                     pl.BlockSpec((1, LANES), lambda bi, p, i: (0, 0))]
        args += [a_lambda, a_norm_g]
    return pl.pallas_call(
        functools.partial(_flash_causal_kernel, diff=diff, has_mask=mask is not None, col0=bias_col0),
        grid=(b, npair // pp, t_all // t),
        in_specs=in_specs,
        out_specs=pl.BlockSpec((1, t, w), lambda bi, p, i: (bi, i, p)),
        out_shape=jax.ShapeDtypeStruct((b, t_all, npair * LANES), F32),
        scratch_shapes=[pltpu.VMEM((2 * pp, t, 1), F32), pltpu.VMEM((2 * pp, t, 1), F32),
                        pltpu.VMEM((2 * pp, t, LANES), F32)],
        compiler_params=_cparams(("arbitrary", "arbitrary", "arbitrary")),
        name="flash_diff" if diff else "flash_sel",
    )(*args)


def _flash_full_kernel(*refs, diff, has_mask, tk):
    lam_ref, q_ref, kc_ref, vc_ref, kn_ref, vn_ref, bias_ref = refs[:7]
    pos = 7
    mask_ref = None
    if has_mask:
        mask_ref = refs[pos]
        pos += 1
    alam_ref = g_ref = None
    if diff:
        alam_ref, g_ref = refs[pos], refs[pos + 1]
        pos += 2
    o_ref, m_ref, l_ref, acc_ref = refs[pos:pos + 4]
    past = kc_ref.shape[2]
    tn = kn_ref.shape[1]
    nb = bias_ref.shape[0]
    lane, qs = _flash_init(q_ref, m_ref, l_ref, acc_ref)

    def tile(kb, vb, c0, width):
        add = mask_ref[0, :, c0:c0 + width].astype(F32) if has_mask else None
        _flash_tile(qs, [kb], [vb], [bias_ref[min(m, nb - 1), :, c0:c0 + width] for m in range(2)], (None, None),
                    add, None, m_ref, l_ref, acc_ref)

    for j in range(past // tk):
        tile(kc_ref[0, 0, j * tk:(j + 1) * tk, :].astype(BF16), vc_ref[0, 0, j * tk:(j + 1) * tk, :].astype(BF16),
             j * tk, tk)
    tile(kn_ref[0].astype(BF16), vn_ref[0].astype(BF16), past, tn)
    _flash_finish(diff, lam_ref, alam_ref, g_ref, o_ref, l_ref, acc_ref, lane)


def flash_full(proj, k_cache, v_cache, layer, full_bias, lam_init, *, diff, qcol, kcol, vcol, bias_col0,
               mask=None, a_lambda=None, a_norm_g=None):
    b, tq, _ = proj.shape
    past = k_cache.shape[2]
    npair = 4
    nb = 1 if diff else 2
    ltot = past + tq
    qb, kb, vb = qcol // LANES, kcol // LANES, vcol // LANES
    bb0 = bias_col0 // nb
    in_specs = [pl.BlockSpec(memory_space=pltpu.SMEM),
                pl.BlockSpec((1, tq, LANES), lambda bi, p: (bi, 0, qb + p)),
                pl.BlockSpec((1, 1, past, LANES), lambda bi, p: (layer, bi, 0, p)),
                pl.BlockSpec((1, 1, past, LANES), lambda bi, p: (layer, bi, 0, p)),
                pl.BlockSpec((1, tq, LANES), lambda bi, p: (bi, 0, kb + p)),
                pl.BlockSpec((1, tq, LANES), lambda bi, p: (bi, 0, vb + p)),
                pl.BlockSpec((nb, tq, ltot), lambda bi, p: (bb0 + p, 0, 0))]
    args = [lam_init, proj, k_cache, v_cache, proj, proj, full_bias]
    if mask is not None:
        in_specs.append(pl.BlockSpec((1, tq, mask.shape[2]), lambda bi, p: (bi, 0, 0)))
        args.append(mask)
    if diff:
        in_specs += [pl.BlockSpec((4, A_DIM), lambda bi, p: (0, 0)),
                     pl.BlockSpec((1, LANES), lambda bi, p: (0, 0))]
        args += [a_lambda, a_norm_g]
    return pl.pallas_call(
        functools.partial(_flash_full_kernel, diff=diff, has_mask=mask is not None, tk=past),
        grid=(b, npair),
        in_specs=in_specs,
        out_specs=pl.BlockSpec((1, tq, LANES), lambda bi, p: (bi, 0, p)),
        out_shape=jax.ShapeDtypeStruct((b, tq, npair * LANES), F32),
        scratch_shapes=[pltpu.VMEM((2, tq, 1), F32), pltpu.VMEM((2, tq, 1), F32), pltpu.VMEM((2, tq, LANES), F32)],
        compiler_params=_cparams(("arbitrary", "arbitrary")),
        name="flash_diff_full" if diff else "flash_sel_full",
    )(*args)


def _index_operands(qi, misc, width):
    tq = qi.shape[0]
    w = misc[:, MISC_IW:MISC_IW + IDX_HEADS] * (IDX_HEADS ** -0.5 * IDX_DIM ** -0.5)
    qh = [qi[:, h * IDX_DIM:(h + 1) * IDX_DIM].astype(BF16) for h in range(IDX_HEADS)]
    wh = [jnp.broadcast_to(w[:, h:h + 1], (tq, width)) for h in range(IDX_HEADS)]
    return qh, wh


def _index_scores(qh, wh, kb):
    isc = None
    for q, w in zip(qh, wh):
        term = w[:, :kb.shape[0]] * jnp.maximum(_dot_nt(q, kb), 0.0)
        isc = term if isc is None else isc + term
    return isc + 0.0


def _sortable(x):
    bits = lax.bitcast_convert_type(x, jnp.int32)
    return jnp.where(bits < 0, bits ^ jnp.int32(0x7FFFFFFF), bits)


def _store_keys(keys_ref, hi_ref, cs, key):
    width = key.shape[1]
    keys_ref[:, pl.ds(cs, width)] = key
    hi_ref[:, pl.ds(cs, width)] = jnp.right_shift(key, 16).astype(jnp.int16)


def _select_topk(keys_ref, hi_ref, lo_ref, nkb, k_sel, out_ref):
    tq = keys_ref.shape[0]

    def block(ref, j):
        return ref[:, pl.ds(pl.multiple_of(j * SEL_KB, SEL_KB), SEL_KB)]

    def count_ge(ref, cand):
        cb = jnp.broadcast_to(cand, (tq, SEL_KB)).astype(jnp.int16)

        def body(j, acc):
            return acc + jnp.where(block(ref, j) >= cb, jnp.int16(1), jnp.int16(0))

        acc = lax.fori_loop(0, nkb, body, jnp.zeros((tq, SEL_KB), jnp.int16))
        return jnp.sum(acc.astype(jnp.int32), axis=-1, keepdims=True)

    def count_gt(ref, t):
        top = -I16_MIN - 1
        return jnp.where(t >= top, 0, count_ge(ref, jnp.minimum(t + 1, top)))

    def search(ref, base):
        zero = jnp.zeros((tq, 1), jnp.int32)
        t0 = jnp.where(base + count_ge(ref, zero) >= k_sel, zero, jnp.full((tq, 1), I16_MIN, jnp.int32))

        def bit_body(it, t):
            cand = t + jnp.left_shift(jnp.int32(1), 14 - it)
            return jnp.where(base + count_ge(ref, cand) >= k_sel, cand, t)

        return lax.fori_loop(0, 15, bit_body, t0)

    hi = search(hi_ref, jnp.zeros((tq, 1), jnp.int32))
    above = count_gt(hi_ref, hi)
    hib = jnp.broadcast_to(hi, (tq, SEL_KB))

    def lo_body(j, carry):
        key = block(keys_ref, j)
        lo = jnp.bitwise_and(key, 0xFFFF) + I16_MIN
        lo = jnp.where(jnp.right_shift(key, 16) == hib, lo, I16_MIN)
        lo_ref[:, pl.ds(pl.multiple_of(j * SEL_KB, SEL_KB), SEL_KB)] = lo.astype(jnp.int16)
        return carry

    lax.fori_loop(0, nkb, lo_body, 0)
    lo = search(lo_ref, above)
    thr = hi * 65536 + (lo - I16_MIN)
    need = k_sel - (above + count_gt(lo_ref, lo))
    need = jnp.where(thr == INT_MIN, 0, need).astype(F32)
    thrb = jnp.broadcast_to(thr, (tq, SEL_KB))
    needb = jnp.broadcast_to(need, (tq, SEL_KB))
    r = lax.broadcasted_iota(jnp.int32, (SEL_KB, 2 * SEL_KB), 0)
    c = lax.broadcasted_iota(jnp.int32, (SEL_KB, 2 * SEL_KB), 1)
    tri = jnp.where((r <= c) | (c >= SEL_KB), 1.0, 0.0).astype(BF16)

    def mask_body(j, offs):
        blk = block(keys_ref, j)
        eq = blk == thrb
        cnt = _dot(jnp.where(eq, 1.0, 0.0).astype(BF16), tri)
        sel = (blk > thrb) | (eq & (offs + cnt[:, :SEL_KB] <= needb))
        out_ref[0, :, pl.ds(pl.multiple_of(j * SEL_KB, SEL_KB), SEL_KB)] = jnp.where(sel, 0.0, NEG).astype(out_ref.dtype)
        return offs + cnt[:, SEL_KB:]

    lax.fori_loop(0, nkb, mask_body, jnp.zeros((tq, SEL_KB), F32))


def _tree_sum(parts):
    while len(parts) > 1:
        parts = [parts[k] + parts[k + 1] for k in range(0, len(parts) - 1, 2)] + parts[len(parts) & ~1:]
    return parts[0]


def _select_causal_kernel(iq_ref, mq_ref, mk_ref, o_ref, keys_ref, hi_ref, lo_ref, *, k_sel):
    tq = iq_ref.shape[1]
    i = pl.program_id(1)
    nkb = (i * tq) // SEL_ROWS + 1
    sub = 16
    qt = jnp.transpose(iq_ref[0])
    mt = jnp.transpose(mq_ref[0])
    qh = [qt[h * IDX_DIM:(h + 1) * IDX_DIM, :].astype(BF16) for h in range(IDX_HEADS)]
    wh = [mt[MISC_IW + h:MISC_IW + h + 1, :] * (IDX_HEADS ** -0.5 * IDX_DIM ** -0.5) for h in range(IDX_HEADS)]
    qpos = lax.broadcasted_iota(jnp.int32, (SEL_ROWS, tq), 1) + i * tq
    kpos = lax.broadcasted_iota(jnp.int32, (SEL_ROWS, tq), 0)

    def rows(j):
        return pl.ds(pl.multiple_of(j * SEL_ROWS, SEL_ROWS), SEL_ROWS)

    def score_body(j, carry):
        kb = mk_ref[0, rows(j), :][:, 0:IDX_DIM].astype(BF16)
        isc = _tree_sum([w * jnp.maximum(_dot(kb, q), 0.0) for q, w in zip(qh, wh)]) + 0.0
        vis = _chunk_of(kpos + j * SEL_ROWS) <= _chunk_of(qpos)
        key = jnp.where(vis, _sortable(isc), INT_MIN)
        keys_ref[rows(j), :] = key
        hi_ref[rows(j), :] = jnp.right_shift(key, 16).astype(jnp.int16)
        return carry

    lax.fori_loop(0, nkb, score_body, 0)

    def count_ge(ref, cand):
        cb = jnp.broadcast_to(cand, (sub, tq)).astype(jnp.int16)

        def body(j, acc):
            blk = ref[rows(j), :]
            hits = [jnp.where(blk[k * sub:(k + 1) * sub] >= cb, jnp.int16(1), jnp.int16(0))
                    for k in range(SEL_ROWS // sub)]
            return acc + _tree_sum(hits)

        acc = lax.fori_loop(0, nkb, body, jnp.zeros((sub, tq), jnp.int16))
        return jnp.sum(acc.astype(jnp.int32), axis=0, keepdims=True)

    def count_gt(ref, t):
        top = -I16_MIN - 1
        return jnp.where(t >= top, 0, count_ge(ref, jnp.minimum(t + 1, top)))

    def search(ref, base):
        zero = jnp.zeros((1, tq), jnp.int32)
        t0 = jnp.where(base + count_ge(ref, zero) >= k_sel, zero, jnp.full((1, tq), I16_MIN, jnp.int32))

        def bit_body(it, t):
            cand = t + jnp.left_shift(jnp.int32(1), 14 - it)
            return jnp.where(base + count_ge(ref, cand) >= k_sel, cand, t)

        return lax.fori_loop(0, 15, bit_body, t0)

    hi = search(hi_ref, jnp.zeros((1, tq), jnp.int32))
    above = count_gt(hi_ref, hi)

    def lo_body(j, carry):
        key = keys_ref[rows(j), :]
        lo = jnp.bitwise_and(key, 0xFFFF) + I16_MIN
        lo = jnp.where(jnp.right_shift(key, 16) == hi, lo, I16_MIN)
        lo_ref[rows(j), :] = lo.astype(jnp.int16)
        return carry

    lax.fori_loop(0, nkb, lo_body, 0)
    lo = search(lo_ref, above)
    thr = hi * 65536 + (lo - I16_MIN)
    need = k_sel - (above + count_gt(lo_ref, lo))
    need = jnp.where(thr == INT_MIN, 0, need).astype(F32)
    r = lax.broadcasted_iota(jnp.int32, (2 * SEL_PRE, SEL_PRE), 0)
    c = lax.broadcasted_iota(jnp.int32, (2 * SEL_PRE, SEL_PRE), 1)
    tri = jnp.where((c <= r) | (r >= SEL_PRE), 1.0, 0.0).astype(BF16)
    r = lax.broadcasted_iota(jnp.int32, (tq, tq), 0)
    c = lax.broadcasted_iota(jnp.int32, (tq, tq), 1)
    eye = jnp.where(r == c, 1.0, 0.0).astype(BF16)

    def mask_body(j, offs):
        blk = keys_ref[rows(j), :]
        parts = [blk[k * SEL_PRE:(k + 1) * SEL_PRE] for k in range(SEL_ROWS // SEL_PRE)]
        eqs = [p == thr for p in parts]
        cnts = [_dot(tri, jnp.where(eq, 1.0, 0.0).astype(BF16)) for eq in eqs]
        sels = []
        for p, eq, cnt in zip(parts, eqs, cnts):
            sel = (p > thr) | (eq & (offs + cnt[:SEL_PRE] <= need))
            sels.append(jnp.where(sel, 1.0, 0.0).astype(BF16))
            offs = offs + cnt[SEL_PRE:SEL_PRE + 1]
        picked = _dot_nt(eye, jnp.concatenate(sels, axis=0))
        o_ref[0, :, rows(j)] = ((picked - 1.0) * -NEG).astype(o_ref.dtype)
        return offs

    lax.fori_loop(0, nkb, mask_body, jnp.zeros((1, tq), F32))


def select_causal(proj):
    b, t, _ = proj.shape
    tq = SEL_TQ
    k_sel = min(TOPK_MAX, t // 4)
    return pl.pallas_call(
        functools.partial(_select_causal_kernel, k_sel=k_sel),
        grid=(b, t // tq),
        in_specs=[pl.BlockSpec((1, tq, IDX_HEADS * IDX_DIM), lambda bi, i: (bi, i, COL_IQ // (IDX_HEADS * IDX_DIM))),
                  pl.BlockSpec((1, tq, LANES), lambda bi, i: (bi, i, COL_MISC // LANES)),
                  pl.BlockSpec((1, t, LANES), lambda bi, i: (bi, 0, COL_MISC // LANES))],
        out_specs=pl.BlockSpec((1, tq, t), lambda bi, i: (bi, i, 0)),
        out_shape=jax.ShapeDtypeStruct((b, t, t), BF16),
        scratch_shapes=[pltpu.VMEM((t, tq), jnp.int32), pltpu.VMEM((t, tq), jnp.int16),
                        pltpu.VMEM((t, tq), jnp.int16)],
        compiler_params=_cparams(("arbitrary", "arbitrary")),
        name="select_causal",
    )(proj, proj, proj)


def _select_full_kernel(iq_ref, mq_ref, kc_ref, o_ref, keys_ref, hi_ref, lo_ref, *, k_sel):
    tq = iq_ref.shape[1]
    past = kc_ref.shape[2]
    nkb = keys_ref.shape[1] // SEL_KB
    misc = mq_ref[0]
    qh, wh = _index_operands(iq_ref[0], misc, SEL_KB)
    for j in range(past // SEL_KB):
        kb = kc_ref[0, 0, j * SEL_KB:(j + 1) * SEL_KB, :].astype(BF16)
        _store_keys(keys_ref, hi_ref, j * SEL_KB, _sortable(_index_scores(qh, wh, kb)))
    _store_keys(keys_ref, hi_ref, past, jnp.full((tq, keys_ref.shape[1] - past), INT_MIN, jnp.int32))
    _store_keys(keys_ref, hi_ref, past, _sortable(_index_scores(qh, wh, misc[:, 0:IDX_DIM].astype(BF16))))
    _select_topk(keys_ref, hi_ref, lo_ref, nkb, k_sel, o_ref)


def select_full(proj, kidx_cache, layer):
    b, tq, _ = proj.shape
    past = kidx_cache.shape[2]
    ltot = past + tq
    lpad = -(-ltot // SEL_KB) * SEL_KB
    k_sel = min(TOPK_MAX, ltot // 4)
    return pl.pallas_call(
        functools.partial(_select_full_kernel, k_sel=k_sel),
        grid=(b,),
        in_specs=[pl.BlockSpec((1, tq, IDX_HEADS * IDX_DIM), lambda bi: (bi, 0, COL_IQ // (IDX_HEADS * IDX_DIM))),
                  pl.BlockSpec((1, tq, LANES), lambda bi: (bi, 0, COL_MISC // LANES)),
                  pl.BlockSpec((1, 1, past, IDX_DIM), lambda bi: (layer, bi, 0, 0))],
        out_specs=pl.BlockSpec((1, tq, lpad), lambda bi: (bi, 0, 0)),
        out_shape=jax.ShapeDtypeStruct((b, tq, lpad), BF16),
        scratch_shapes=[pltpu.VMEM((tq, lpad), jnp.int32), pltpu.VMEM((tq, lpad), jnp.int16),
                        pltpu.VMEM((tq, lpad), jnp.int16)],
        compiler_params=_cparams(("arbitrary",)),
        name="select_full",
    )(proj, proj, kidx_cache)


CONVB_ROWS = 64
CONVB_HEAD = 32


def _convb_kernel(a_ref, gt_ref, init_ref, w_ref, dwb_ref, lng_ref, lnb_ref, o_ref, tail_ref, f_ref, sh_ref):
    tt = a_ref.shape[1]
    lead = CONVB_HEAD - (B_WIDTH - 1)

    @pl.when(pl.program_id(1) == 0)
    def _():
        f_ref[0:CONVB_HEAD, :] = init_ref[0]

    f_ref[CONVB_HEAD:CONVB_HEAD + tt, :] = a_ref[0] * jax.nn.sigmoid(gt_ref[0])
    span = sh_ref.shape[1]
    for ph in range(1, SUBLANES):
        sh_ref[ph] = f_ref[ph:ph + span, :]
    for r in range(tt // CONVB_ROWS):
        acc = None
        for k in range(B_WIDTH):
            ph = (lead + k) % SUBLANES
            s0 = r * CONVB_ROWS + lead + k - ph
            rows = f_ref[s0:s0 + CONVB_ROWS, :] if ph == 0 else sh_ref[ph, s0:s0 + CONVB_ROWS, :]
            term = w_ref[k:k + 1, :] * rows
            acc = term if acc is None else acc + term
        y = acc + dwb_ref[...]
        yc = y - jnp.mean(y, axis=-1, keepdims=True)
        yn = yc * lax.rsqrt(jnp.mean(yc * yc, axis=-1, keepdims=True) + EPS) * lng_ref[...] + lnb_ref[...]
        o_ref[0, r * CONVB_ROWS:(r + 1) * CONVB_ROWS, :] = _silu(yn)
    last = f_ref[tt:tt + CONVB_HEAD, :]
    tail_ref[0] = last
    f_ref[0:CONVB_HEAD, :] = last


def conv_module(proj, init, w, dwb, lng, lnb, tt):
    b, t, _ = proj.shape
    cb = COL_BGLU // B_CH
    vec = pl.BlockSpec((1, B_CH), lambda bi, j: (0, 0))
    return pl.pallas_call(
        _convb_kernel,
        grid=(b, t // tt),
        in_specs=[pl.BlockSpec((1, tt, B_CH), lambda bi, j: (bi, j, cb)),
                  pl.BlockSpec((1, tt, B_CH), lambda bi, j: (bi, j, cb + 1)),
                  pl.BlockSpec((1, CONVB_HEAD, B_CH), lambda bi, j: (bi, 0, 0)),
                  pl.BlockSpec((B_WIDTH, B_CH), lambda bi, j: (0, 0)),
                  vec, vec, vec],
        out_specs=[pl.BlockSpec((1, tt, B_CH), lambda bi, j: (bi, j, 0)),
                   pl.BlockSpec((1, CONVB_HEAD, B_CH), lambda bi, j: (bi, 0, 0))],
        out_shape=[jax.ShapeDtypeStruct((b, t, B_CH), F32), jax.ShapeDtypeStruct((b, CONVB_HEAD, B_CH), F32)],
        scratch_shapes=[pltpu.VMEM((CONVB_HEAD + tt, B_CH), F32),
                        pltpu.VMEM((SUBLANES, tt + CONVB_HEAD - SUBLANES, B_CH), F32)],
        compiler_params=_cparams(("arbitrary", "arbitrary")),
        name="conv_module",
    )(proj, proj, init, w, dwb, lng, lnb)


GDN_HEAD = 8
GDN_BB = 4
INV_BASE = 8


def _unit_lower_inverses(mats):
    n = mats[0].shape[0]
    r = lax.broadcasted_iota(jnp.int32, (n, n), 0)
    c = lax.broadcasted_iota(jnp.int32, (n, n), 1)
    eye = jnp.where(r == c, 1.0, 0.0)

    def same_block(size):
        shift = int(math.log2(size))
        return jnp.right_shift(r, shift) == jnp.right_shift(c, shift)

    diag = same_block(INV_BASE)
    xs = [jnp.where(diag, -a, 0.0) for a in mats]
    ps = [eye + x for x in xs]
    splits = [_split_bf16(x) for x in xs]
    for _ in range(int(math.log2(INV_BASE)) - 1):
        splits = [_split_bf16(_dot_split(s, s)) for s in splits]
        ps = [p + _dot_split(_split_bf16(p), s) for p, s in zip(ps, splits)]
    size = INV_BASE
    while size < n:
        off = same_block(2 * size) & jnp.logical_not(same_block(size))
        lows = [_split_bf16(jnp.where(off, a, 0.0)) for a in mats]
        psplit = [_split_bf16(p) for p in ps]
        mids = [_split_bf16(_dot_split(lo, p)) for lo, p in zip(lows, psplit)]
        ps = [p - _dot_split(ph, mid) for p, ph, mid in zip(ps, psplit, mids)]
        size *= 2
    return ps


def _split_bf16(x):
    hi = x.astype(BF16)
    return hi, (x - hi.astype(F32)).astype(BF16)


def _dot_split(a, b):
    (ah, al), (bh, bl) = a, b
    return _dot(ah, bh) + (_dot(ah, bl) + _dot(al, bh))


def _gdn_kernel(x_ref, z_ref, misc_ref, cinit_ref, s0_ref, cw_ref, alog_ref, dtb_ref, ng_ref,
                o_ref, tail_ref, sout_ref, f_ref, s_ref):
    c = pl.program_id(1)

    @pl.when(c == 0)
    def _():
        f_ref[:, 0:GDN_HEAD, :] = cinit_ref[...]
        s_ref[...] = s0_ref[...]

    bb, cc = x_ref.shape[0], x_ref.shape[1]
    lead = GDN_HEAD - (D_CONV - 1)
    r = lax.broadcasted_iota(jnp.int32, (cc, cc), 0)
    col = lax.broadcasted_iota(jnp.int32, (cc, cc), 1)
    incl = col <= r
    strict = col < r
    lower = jnp.where(incl, 1.0, 0.0)

    ys, betas, gcums, gcum_ts = [], [], [], []
    for bi in range(bb):
        u = x_ref[bi]
        f_ref[bi, GDN_HEAD:GDN_HEAD + cc, :] = u
        y = None
        for j in range(D_CONV):
            term = cw_ref[j:j + 1, :] * f_ref[bi, lead + j:lead + j + cc, :]
            y = term if y is None else y + term
        ys.append(_silu(y))
        last = u[cc - GDN_HEAD:cc, :]
        tail_ref[bi] = last
        f_ref[bi, 0:GDN_HEAD, :] = last
        misc = misc_ref[bi]
        betas.append(jax.nn.sigmoid(misc))
        xg = misc + dtb_ref[...]
        softplus = jnp.maximum(xg, 0.0) + jnp.log(1.0 + jnp.exp(-jnp.abs(xg)))
        g_all = -jnp.exp(alog_ref[...]) * softplus
        gcum = _dot(lower, g_all, HIGHEST)
        gcums.append(gcum)
        gcum_ts.append(jnp.transpose(gcum))

    prob = [(bi, h) for bi in range(bb) for h in range(D_HEADS)]
    qn, kn, vh, bc, gc, gl, decay = [], [], [], [], [], [], []
    for bi, h in prob:
        y = ys[bi]
        q = y[:, h * D_KDIM:(h + 1) * D_KDIM]
        k = y[:, D_QK + h * D_KDIM:D_QK + (h + 1) * D_KDIM]
        vh.append(y[:, 2 * D_QK + h * D_VDIM:2 * D_QK + (h + 1) * D_VDIM])
        qn.append(q * lax.rsqrt(jnp.sum(q * q, axis=-1, keepdims=True) + EPS) * (D_KDIM ** -0.5))
        kn.append(k * lax.rsqrt(jnp.sum(k * k, axis=-1, keepdims=True) + EPS))
        bc.append(betas[bi][:, MISC_DB + h:MISC_DB + h + 1])
        g_col = gcums[bi][:, MISC_DA + h:MISC_DA + h + 1]
        g_row = gcum_ts[bi][MISC_DA + h:MISC_DA + h + 1, :]
        gc.append(g_col)
        gl.append(gcums[bi][cc - 1:cc, MISC_DA + h:MISC_DA + h + 1])
        decay.append(jnp.where(incl, jnp.exp(jnp.where(incl, g_col - g_row, 0.0)), 0.0))
    n = len(prob)
    kb = [x.astype(BF16) for x in kn]
    qb = [x.astype(BF16) for x in qn]
    kk = [_dot_nt(kb[g], kb[g]) for g in range(n)]
    tinv = _unit_lower_inverses([jnp.where(strict, bc[g] * decay[g] * kk[g], 0.0) for g in range(n)])
    s = [s_ref[bi, h] for bi, h in prob]
    sb = [x.astype(BF16) for x in s]
    eg = [jnp.exp(x) for x in gc]
    ks = [_dot(kb[g], sb[g]) for g in range(n)]
    uu = [_dot_split(_split_bf16(tinv[g]), _split_bf16(bc[g] * (vh[g] - eg[g] * ks[g]))) for g in range(n)]
    ub = [x.astype(BF16) for x in uu]
    qk = [(_dot_nt(qb[g], kb[g]) * decay[g]).astype(BF16) for g in range(n)]
    qs = [_dot(qb[g], sb[g]) for g in range(n)]
    o = [eg[g] * qs[g] + _dot(qk[g], ub[g]) for g in range(n)]
    kd = [(kn[g] * jnp.exp(gl[g] - gc[g])).astype(BF16) for g in range(n)]
    s_new = [jnp.exp(gl[g]) * s[g] + _dot_tn(kd[g], ub[g]) for g in range(n)]
    for g, (bi, h) in enumerate(prob):
        s_ref[bi, h] = s_new[g]
        zh = z_ref[bi, :, h * D_VDIM:(h + 1) * D_VDIM]
        o_ref[bi, :, h * D_VDIM:(h + 1) * D_VDIM] = _rms(o[g], ng_ref[...]) * _silu(zh)

    @pl.when(c == pl.num_programs(1) - 1)
    def _():
        sout_ref[...] = s_ref[...]


def gated_delta(proj, cinit, s0, cw, alog, dtb, ng):
    b, t, _ = proj.shape
    cc = min(t, CHUNK)
    bb = math.gcd(b, GDN_BB)
    vec = pl.BlockSpec((1, LANES), lambda bi, c: (0, 0))
    return pl.pallas_call(
        _gdn_kernel,
        grid=(b // bb, t // cc),
        in_specs=[pl.BlockSpec((bb, cc, D_CONV_CH), lambda bi, c: (bi, c, COL_DQKV // D_CONV_CH)),
                  pl.BlockSpec((bb, cc, D_V), lambda bi, c: (bi, c, COL_DZ // D_V)),
                  pl.BlockSpec((bb, cc, LANES), lambda bi, c: (bi, c, COL_MISC // LANES)),
                  pl.BlockSpec((bb, GDN_HEAD, D_CONV_CH), lambda bi, c: (bi, 0, 0)),
                  pl.BlockSpec((bb, D_HEADS, D_KDIM, D_VDIM), lambda bi, c: (bi, 0, 0, 0)),
                  pl.BlockSpec((D_CONV, D_CONV_CH), lambda bi, c: (0, 0)),
                  vec, vec, vec],
        out_specs=[pl.BlockSpec((bb, cc, D_V), lambda bi, c: (bi, c, 0)),
                   pl.BlockSpec((bb, GDN_HEAD, D_CONV_CH), lambda bi, c: (bi, 0, 0)),
                   pl.BlockSpec((bb, D_HEADS, D_KDIM, D_VDIM), lambda bi, c: (bi, 0, 0, 0))],
        out_shape=[jax.ShapeDtypeStruct((b, t, D_V), F32),
                   jax.ShapeDtypeStruct((b, GDN_HEAD, D_CONV_CH), F32),
                   jax.ShapeDtypeStruct((b, D_HEADS, D_KDIM, D_VDIM), F32)],
        scratch_shapes=[pltpu.VMEM((bb, GDN_HEAD + cc, D_CONV_CH), F32),
                        pltpu.VMEM((bb, D_HEADS, D_KDIM, D_VDIM), F32)],
        compiler_params=_cparams(("arbitrary", "arbitrary")),
        name="gated_delta",
    )(proj, proj, proj, cinit, s0, cw, alog, dtb, ng)


def _merge_kernel(oa_ref, ob_ref, oc_ref, od_ref, g0_ref, g1_ref, g2_ref, g3_ref, x_ref, gt_ref, ng_ref,
                  wbr_ref, wout_ref, o_ref):
    bt, tt, d = x_ref.shape
    rows = bt * tt
    merged = None
    for m, (br, gate) in enumerate(((oa_ref, g0_ref), (ob_ref, g1_ref), (oc_ref, g2_ref), (od_ref, g3_ref))):
        term = jax.nn.sigmoid(gate[...].reshape(rows, d)) * _dot(br[...].reshape(rows, BRANCH_W).astype(BF16), wbr_ref[m])
        merged = term if merged is None else merged + term
    mix = _dot(merged.astype(BF16), wout_ref[...])
    o_ref[...] = x_ref[...] + gt_ref[...] * _rms(mix, ng_ref[...]).reshape(bt, tt, d)


def merge_out(branches, proj, x, mod, ng, wbr, wout, bt, tt):
    b, t, d = x.shape
    gb = COL_GATE // d
    br_spec = pl.BlockSpec((bt, tt, BRANCH_W), lambda i, j: (i, j, 0))
    gate_specs = [pl.BlockSpec((bt, tt, d), functools.partial(lambda i, j, m: (i, j, gb + m), m=m))
                  for m in range(N_BRANCH)]
    return pl.pallas_call(
        _merge_kernel,
        grid=(b // bt, t // tt),
        in_specs=[br_spec] * 4 + gate_specs + [
            pl.BlockSpec((bt, tt, d), lambda i, j: (i, j, 0)),
            pl.BlockSpec((bt, 1, d), lambda i, j: (i, 0, 2)),
            pl.BlockSpec((1, d), lambda i, j: (0, 0)),
            pl.BlockSpec((N_BRANCH, BRANCH_W, d), lambda i, j: (0, 0, 0)),
            pl.BlockSpec((d, d), lambda i, j: (0, 0))],
        out_specs=pl.BlockSpec((bt, tt, d), lambda i, j: (i, j, 0)),
        out_shape=jax.ShapeDtypeStruct((b, t, d), F32),
        compiler_params=_cparams(("arbitrary", "arbitrary")),
        name="merge_out",
    )(*branches, proj, proj, proj, proj, x, mod, ng, wbr, wout)


def _mlp_kernel(x_ref, sh_ref, sc_ref, gt_ref, g2_ref, g3_ref, w1_ref, w2_ref, o_ref, h_ref, acc_ref):
    bt, tt, d = x_ref.shape
    f = pl.program_id(2)

    @pl.when(f == 0)
    def _():
        h = _rms(x_ref[...], g2_ref[...]) * (1.0 + sc_ref[...]) + sh_ref[...]
        h_ref[...] = h.reshape(bt * tt, d).astype(BF16)
        acc_ref[...] = jnp.zeros(acc_ref.shape, F32)

    a = jnp.maximum(_dot(h_ref[...], w1_ref[...]), 0.0)
    acc_ref[...] += _dot((a * a).astype(BF16), w2_ref[...])

    @pl.when(f == pl.num_programs(2) - 1)
    def _():
        o_ref[...] = x_ref[...] + gt_ref[...] * _rms(acc_ref[...], g3_ref[...]).reshape(bt, tt, d)


def mlp(x, mod, g2, g3, w1, w2, bt, tt):
    b, t, d = x.shape
    ff = w1.shape[1]
    tf = 1024
    return pl.pallas_call(
        _mlp_kernel,
        grid=(b // bt, t // tt, ff // tf),
        in_specs=[pl.BlockSpec((bt, tt, d), lambda i, j, f: (i, j, 0)),
                  pl.BlockSpec((bt, 1, d), lambda i, j, f: (i, 0, 3)),
                  pl.BlockSpec((bt, 1, d), lambda i, j, f: (i, 0, 4)),
                  pl.BlockSpec((bt, 1, d), lambda i, j, f: (i, 0, 5)),
                  pl.BlockSpec((1, d), lambda i, j, f: (0, 0)),
                  pl.BlockSpec((1, d), lambda i, j, f: (0, 0)),
                  pl.BlockSpec((d, tf), lambda i, j, f: (0, f)),
                  pl.BlockSpec((tf, d), lambda i, j, f: (f, 0))],
        out_specs=pl.BlockSpec((bt, tt, d), lambda i, j, f: (i, j, 0)),
        out_shape=jax.ShapeDtypeStruct((b, t, d), F32),
        scratch_shapes=[pltpu.VMEM((bt * tt, d), BF16), pltpu.VMEM((bt * tt, d), F32)],
        compiler_params=_cparams(("arbitrary", "arbitrary", "arbitrary")),
        name="mlp",
    )(x, mod, mod, mod, g2, g3, w1, w2)


def _combined_in_weight(w_in_l, w_gate_l):
    offs = np.concatenate([[0], np.cumsum(IN_SIZES)])
    (aq, ak, av, bglu, cq, ck, cv, iq, ik, iw, dqkv, dz, db, da) = [
        w_in_l[:, int(offs[i]):int(offs[i + 1])] for i in range(len(IN_SIZES))]
    d = w_in_l.shape[0]
    misc = jnp.concatenate([ik, iw, db, da, jnp.zeros((d, LANES - IDX_DIM - IDX_HEADS - 2 * D_HEADS), F32)], axis=1)
    pad = jnp.zeros((d, COL_GATE - COL_MISC - LANES), F32)
    gates = [w_gate_l[m] for m in range(N_BRANCH)]
    w = jnp.concatenate([aq, ak, av, bglu, cq, ck, cv, dqkv, dz, iq, misc, pad] + gates, axis=1)
    assert w.shape[1] == N_PROJ
    return w.astype(BF16)


def _lane_pad(v, offset):
    return jnp.zeros((1, LANES), F32).at[0, offset:offset + v.shape[0]].set(v)


def _run_group(x, mod, lw, past, bias, rel_bias, tiles):
    b, t, d = x.shape
    bt, tt, conv_tt = tiles
    proj = inproj(x, mod, lw['g'][0:1], lw['w_all'], bt, tt)

    if past is None:
        b_init = jnp.zeros((b, CONVB_HEAD, B_CH), F32)
        d_init = jnp.zeros((b, GDN_HEAD, D_CONV_CH), F32)
        s0 = jnp.zeros((b, D_HEADS, D_KDIM, D_VDIM), F32)
        o_a = flash_causal(proj, bias, rel_bias, lw['lam_init'], diff=True, qcol=COL_AQ, kcol=COL_AK, vcol=COL_AV,
                           bias_col0=0, a_lambda=lw['a_lambda'], a_norm_g=lw['a_norm_g'])
        sel = select_causal(proj)
        o_c = flash_causal(proj, bias, rel_bias, lw['lam_init'], diff=False, qcol=COL_CQ, kcol=COL_CK, vcol=COL_CV,
                           bias_col0=A_HEADS, mask=sel)
    else:
        layer = past['layer']
        lead_b = CONVB_HEAD - (B_WIDTH - 1)
        b_init = jnp.pad(past['b_conv'][layer], ((0, 0), (lead_b, 0), (0, 0)))
        d_init = jnp.pad(past['d_conv'][layer], ((0, 0), (GDN_HEAD - (D_CONV - 1), 0), (0, 0)))
        s0 = past['d_state'][layer]
        o_a = flash_full(proj, past['a_k'], past['a_v'], layer, bias, lw['lam_init'], diff=True,
                         qcol=COL_AQ, kcol=COL_AK, vcol=COL_AV, bias_col0=0,
                         a_lambda=lw['a_lambda'], a_norm_g=lw['a_norm_g'])
        sel = select_full(proj, past['c_kidx'], layer)
        o_c = flash_full(proj, past['c_k'], past['c_v'], layer, bias, lw['lam_init'], diff=False,
                         qcol=COL_CQ, kcol=COL_CK, vcol=COL_CV, bias_col0=A_HEADS, mask=sel)

    o_b, b_tail = conv_module(proj, b_init, lw['b_dw_w'], lw['b_dw_b'], lw['b_ln_g'], lw['b_ln_b'], conv_tt)
    o_d, d_tail, s_new = gated_delta(proj, d_init, s0, lw['d_conv_w'], lw['alog'], lw['dtb'], lw['d_norm_g'])

    mbt, mtt = (1, 256) if t >= 256 else (min(b, 256 // t), t)
    x1 = merge_out((o_a, o_b, o_c, o_d), proj, x, mod, lw['g'][1:2], lw['w_br'], lw['w_out'], mbt, mtt)
    fbt, ftt = (1, 1024) if t >= 1024 else (min(b, 1024 // t), t)
    x2 = mlp(x1, mod, lw['g'][2:3], lw['g'][3:4], lw['w1'], lw['w2'], fbt, ftt)

    new = {
        'a_k': proj[:, :, COL_AK:COL_AK + A_QK].reshape(b, t, A_HEADS, 2 * A_DIM),
        'a_v': proj[:, :, COL_AV:COL_AV + A_QK].reshape(b, t, A_HEADS, 2 * A_DIM),
        'c_k': proj[:, :, COL_CK:COL_CK + C_W].reshape(b, t, C_HEADS, C_DIM),
        'c_v': proj[:, :, COL_CV:COL_CV + C_W].reshape(b, t, C_HEADS, C_DIM),
        'c_kidx': proj[:, :, COL_MISC:COL_MISC + IDX_DIM],
        'b_conv': b_tail[:, CONVB_HEAD - (B_WIDTH - 1):, :],
        'd_conv': d_tail[:, GDN_HEAD - (D_CONV - 1):, :],
        'd_state': s_new,
    }
    return x2, new


def kernel(x_prompt, x_sample, c_prompt, c_sample, cache_a_k, cache_a_v, cache_c_k, cache_c_v, cache_c_kidx, state_b_conv, state_d_conv, state_d_state, rel_bias, ada_w, ada_b, norm_g, w_in, a_lambda, a_norm_g, b_dw_w, b_dw_b, b_ln_g, b_ln_b, d_conv_w, d_a_log, d_dt_bias, d_norm_g, w_gate, w_br, w_out, mlp_w1, mlp_w2):
    depth = w_in.shape[0]
    bp, tp, d = x_prompt.shape
    bs, ts, _ = x_sample.shape
    past_len = cache_a_k.shape[2]

    mod_all = adaln_mod(jnp.concatenate([c_prompt, c_sample], axis=0), ada_w, ada_b)
    bias_p = bias_tiles(rel_bias, FLASH_T, 2 * FLASH_T, FLASH_T, 0)
    bias_s = bias_tiles(rel_bias, ts, past_len + ts, past_len, 0)

    past = {
        'a_k': cache_a_k.reshape(depth, bs, past_len, A_QK),
        'a_v': cache_a_v.reshape(depth, bs, past_len, A_QK),
        'c_k': cache_c_k.reshape(depth, bs, past_len, C_W),
        'c_v': cache_c_v.reshape(depth, bs, past_len, C_W),
        'c_kidx': cache_c_kidx, 'b_conv': state_b_conv, 'd_conv': state_d_conv, 'd_state': state_d_state,
    }
    names = ('a_k', 'a_v', 'c_k', 'c_v', 'c_kidx', 'b_conv', 'd_conv', 'd_state')
    new_p = {n: [] for n in names}
    new_s = {n: [] for n in names}
    xp, xs = x_prompt, x_sample
    for l in range(depth):
        lw = {
            'g': norm_g[l],
            'w_all': _combined_in_weight(w_in[l], w_gate[l]),
            'lam_init': jnp.full((1,), 0.8 - 0.6 * math.exp(-0.3 * l), F32),
            'a_lambda': a_lambda[l],
            'a_norm_g': a_norm_g[l].reshape(1, LANES),
            'b_dw_w': b_dw_w[l], 'b_dw_b': b_dw_b[l].reshape(1, B_CH),
            'b_ln_g': b_ln_g[l].reshape(1, B_CH), 'b_ln_b': b_ln_b[l].reshape(1, B_CH),
            'd_conv_w': d_conv_w[l],
            'alog': _lane_pad(d_a_log[l], MISC_DA), 'dtb': _lane_pad(d_dt_bias[l], MISC_DA),
            'd_norm_g': d_norm_g[l].reshape(1, LANES),
            'w_br': w_br[l].astype(BF16), 'w_out': w_out[l].astype(BF16),
            'w1': mlp_w1[l].astype(BF16), 'w2': mlp_w2[l].astype(BF16),
        }
        mod_p = mod_all[l, :bp].reshape(bp, 1, 6 * d)
        mod_s = mod_all[l, bp:].reshape(bs, 1, 6 * d)
        xp, sp = _run_group(xp, mod_p, lw, None, bias_p, rel_bias, (1, 1024, 256))
        xs, ss = _run_group(xs, mod_s, lw, dict(past, layer=l), bias_s, rel_bias, (min(bs, 1024 // ts), ts, ts))
        for n in names:
            new_p[n].append(sp[n])
            new_s[n].append(ss[n])
    return (xp, xs) + tuple(jnp.stack(new_p[n]) for n in names) + tuple(jnp.stack(new_s[n]) for n in names)
```

```python
import functools
import math

import numpy as np
import jax
import jax.numpy as jnp
from jax import lax
from jax.experimental import pallas as pl
from jax.experimental.pallas import tpu as pltpu

F32 = jnp.float32
BF16 = jnp.bfloat16
HIGHEST = lax.Precision.HIGHEST

D_MODEL = 1024
CHUNK = 64
BRANCH_W = D_MODEL // 2
N_BRANCH = 4
A_HEADS = 4
A_DIM = 64
B_CH = BRANCH_W
B_WIDTH = 31
C_HEADS = 8
C_DIM = 64
IDX_HEADS = 4
IDX_DIM = 64
TOPK_MAX = 256
D_HEADS = 4
D_VDIM = 128
D_KDIM = 64
D_CONV = 4
NUM_BUCKETS = 32
MAX_DISTANCE = 128
D_FF = 4 * D_MODEL
EPS = 1e-6
A_QK = A_HEADS * 2 * A_DIM
C_W = C_HEADS * C_DIM
D_QK = D_HEADS * D_KDIM
D_V = D_HEADS * D_VDIM
D_CONV_CH = 2 * D_QK + D_V
IN_SIZES = (A_QK, A_QK, A_QK, 2 * B_CH, C_W, C_W, C_W, IDX_HEADS * IDX_DIM, IDX_DIM, IDX_HEADS,
            D_CONV_CH, D_V, D_HEADS, D_HEADS)

LANES = 128
SUBLANES = 8
NEG = -1e30
LOG2E = 1.4426950408889634
INT_MIN = -2 ** 31
VMEM_LIMIT = 56 * 1024 * 1024

COL_AQ, COL_AK, COL_AV = 0, 512, 1024
COL_BGLU = 1536
COL_CQ, COL_CK, COL_CV = 2560, 3072, 3584
COL_DQKV = 4096
COL_DZ = 5120
COL_IQ = 5632
COL_MISC = 5888
COL_GATE = 6144
N_PROJ = 10240
MISC_IW, MISC_DB, MISC_DA = 64, 68, 72

FLASH_T = 256
FLASH_FAR = 2048
FLASH_PAIRS = 2
SEL_TQ = 512
SEL_KB = 256
SEL_ROWS = 512
SEL_PRE = 256
I16_MIN = -2 ** 15


def _cparams(sem):
    return pltpu.CompilerParams(dimension_semantics=sem, vmem_limit_bytes=VMEM_LIMIT)


def _dot(a, b, precision=None):
    return jnp.dot(a, b, preferred_element_type=F32, precision=precision)


def _dot_nt(a, b):
    return lax.dot_general(a, b, (((1,), (1,)), ((), ())), preferred_element_type=F32)


def _dot_tn(a, b):
    return lax.dot_general(a, b, (((0,), (0,)), ((), ())), preferred_element_type=F32)


def _rms(x, g):
    return x * lax.rsqrt(jnp.mean(x * x, axis=-1, keepdims=True) + EPS) * g


def _silu(x):
    return x * jax.nn.sigmoid(x)


def _chunk_of(pos):
    return jnp.right_shift(pos, int(math.log2(CHUNK)))


def _mod_kernel(c_ref, w_ref, b_ref, o_ref):
    s = _silu(c_ref[...])
    o_ref[0] = _dot(s.astype(BF16), w_ref[0].astype(BF16)) + b_ref[0]


def adaln_mod(c_all, ada_w, ada_b):
    depth, d, n = ada_w.shape
    bc = c_all.shape[0]
    tn = 1024
    return pl.pallas_call(
        _mod_kernel,
        grid=(depth, n // tn),
        in_specs=[pl.BlockSpec((bc, d), lambda l, j: (0, 0)),
                  pl.BlockSpec((1, d, tn), lambda l, j: (l, 0, j)),
                  pl.BlockSpec((1, 1, tn), lambda l, j: (l, 0, j))],
        out_specs=pl.BlockSpec((1, bc, tn), lambda l, j: (l, 0, j)),
        out_shape=jax.ShapeDtypeStruct((depth, bc, n), F32),
        compiler_params=_cparams(("arbitrary", "arbitrary")),
        name="adaln_mod",
    )(c_all, ada_w, ada_b.reshape(depth, 1, n))


def _bias_kernel(tab_ref, o_ref, *, q0, k0):
    h = pl.program_id(0)
    tq, w = o_ref.shape[1], o_ref.shape[2]
    row = lax.broadcasted_iota(jnp.int32, (tq, w), 0)
    col = lax.broadcasted_iota(jnp.int32, (tq, w), 1)
    rel = (col + k0) - (row + q0)
    nb = NUM_BUCKETS // 2
    max_exact = nb // 2
    n = jnp.abs(rel)
    large = max_exact + (jnp.log(jnp.maximum(n, max_exact).astype(F32) / max_exact)
                         / math.log(MAX_DISTANCE / max_exact) * (nb - max_exact)).astype(jnp.int32)
    large = jnp.minimum(large, nb - 1)
    bucket = jnp.where(rel > 0, nb, 0) + jnp.where(n < max_exact, n, large)
    val = jnp.zeros((tq, w), F32)
    for bk in range(NUM_BUCKETS):
        val = jnp.where(bucket == bk, tab_ref[bk, h], val)
    o_ref[0] = val * LOG2E


def bias_tiles(rel_bias, tq, w, q0, k0):
    nh = rel_bias.shape[1]
    return pl.pallas_call(
        functools.partial(_bias_kernel, q0=q0, k0=k0),
        grid=(nh,),
        in_specs=[pl.BlockSpec(memory_space=pltpu.SMEM)],
        out_specs=pl.BlockSpec((1, tq, w), lambda h: (h, 0, 0)),
        out_shape=jax.ShapeDtypeStruct((nh, tq, w), F32),
        compiler_params=_cparams(("arbitrary",)),
        name="bias_tiles",
    )(rel_bias)


def _inproj_kernel(x_ref, sh_ref, sc_ref, g_ref, w_ref, o_ref, h_ref):
    bt, tt, d = x_ref.shape

    @pl.when(pl.program_id(2) == 0)
    def _():
        h = _rms(x_ref[...], g_ref[...]) * (1.0 + sc_ref[...]) + sh_ref[...]
        h_ref[...] = h.reshape(bt * tt, d).astype(BF16)

    o_ref[...] = _dot(h_ref[...], w_ref[...]).reshape(o_ref.shape)


def inproj(x, mod, g, w_all, bt, tt):
    b, t, d = x.shape
    n = w_all.shape[1]
    tn = 2048
    return pl.pallas_call(
        _inproj_kernel,
        grid=(b // bt, t // tt, n // tn),
        in_specs=[pl.BlockSpec((bt, tt, d), lambda i, j, k: (i, j, 0)),
                  pl.BlockSpec((bt, 1, d), lambda i, j, k: (i, 0, 0)),
                  pl.BlockSpec((bt, 1, d), lambda i, j, k: (i, 0, 1)),
                  pl.BlockSpec((1, d), lambda i, j, k: (0, 0)),
                  pl.BlockSpec((d, tn), lambda i, j, k: (0, k))],
        out_specs=pl.BlockSpec((bt, tt, tn), lambda i, j, k: (i, j, k)),
        out_shape=jax.ShapeDtypeStruct((b, t, n), F32),
        scratch_shapes=[pltpu.VMEM((bt * tt, d), BF16)],
        compiler_params=_cparams(("arbitrary", "arbitrary", "arbitrary")),
        name="inproj",
    )(x, mod, mod, g, w_all)


def _flash_tile(qs, kbs, vbs, biases, shifts, add_mask, keep, m_ref, l_ref, acc_ref):
    scores = [_dot_nt(q, kbs[m // 2]) for m, q in enumerate(qs)]
    for m, s in enumerate(scores):
        if biases[m] is not None:
            s = s + biases[m]
        if add_mask is not None:
            s = s + add_mask
        if keep is not None:
            s = jnp.where(keep, s, NEG)
        m_prev = m_ref[m]
        s_max = jnp.max(s, axis=-1, keepdims=True)
        if shifts[m] is not None:
            s_max = s_max + shifts[m]
        m_new = jnp.maximum(m_prev, s_max)
        alpha = jnp.exp2(m_prev - m_new)
        p = jnp.exp2(s - (m_new if shifts[m] is None else m_new - shifts[m]))
        l_ref[m] = alpha * l_ref[m] + jnp.sum(p, axis=-1, keepdims=True)
        acc_ref[m] = alpha * acc_ref[m] + _dot(p.astype(BF16), vbs[m // 2])
        m_ref[m] = m_new


def _flash_finish(diff, lam_ref, alam_ref, g_ref, o_ref, l_ref, acc_ref, lane):
    for j in range(o_ref.shape[2] // LANES):
        o0 = acc_ref[2 * j] / l_ref[2 * j]
        o1 = acc_ref[2 * j + 1] / l_ref[2 * j + 1]
        if diff:
            lv = alam_ref[...]
            lam_init = lam_ref[0]
            lam = (jnp.exp(jnp.sum(lv[0:1] * lv[1:2], axis=-1, keepdims=True))
                   - jnp.exp(jnp.sum(lv[2:3] * lv[3:4], axis=-1, keepdims=True)) + lam_init)
            o = o0 - lam * o1
            o = _rms(o, g_ref[...]) * (1.0 - lam_init)
        else:
            o = jnp.where(lane < C_DIM, o0, o1)
        o_ref[0, :, j * LANES:(j + 1) * LANES] = o


def _flash_init(q_ref, m_ref, l_ref, acc_ref):
    lane = lax.broadcasted_iota(jnp.int32, (1, LANES), 1)
    qs = []
    for j in range(q_ref.shape[2] // LANES):
        q = q_ref[0, :, j * LANES:(j + 1) * LANES] * (A_DIM ** -0.5 * LOG2E)
        qs.append(jnp.where(lane < A_DIM, q, 0.0).astype(BF16))
        qs.append(jnp.where(lane >= A_DIM, q, 0.0).astype(BF16))
    m_ref[...] = jnp.full(m_ref.shape, NEG, F32)
    l_ref[...] = jnp.zeros(l_ref.shape, F32)
    acc_ref[...] = jnp.zeros(acc_ref.shape, F32)
    return lane, qs


def _flash_causal_kernel(*refs, diff, has_mask, col0):
    tab_ref, lam_ref, q_ref, k_ref, v_ref, bias_ref = refs[:6]
    pos = 6
    mask_ref = None
    if has_mask:
        mask_ref = refs[pos]
        pos += 1
    alam_ref = g_ref = None
    if diff:
        alam_ref, g_ref = refs[pos], refs[pos + 1]
        pos += 2
    o_ref, m_ref, l_ref, acc_ref = refs[pos:pos + 4]
    t = q_ref.shape[1]
    npair = q_ref.shape[2] // LANES
    nb = bias_ref.shape[0] // npair
    p = pl.program_id(1)
    i = pl.program_id(2)
    lane, qs = _flash_init(q_ref, m_ref, l_ref, acc_ref)
    bias_of = [nb * (m // 2) + min(m % 2, nb - 1) for m in range(2 * npair)]
    none = (None,) * (2 * npair)
    far = [tab_ref[NUM_BUCKETS // 2 - 1, col0 + nb * npair * p + bias_of[m]] * LOG2E for m in range(2 * npair)]

    def tile(ks, width, biases, shifts, keep):
        kbs = [k_ref[0, pl.ds(ks, width), j * LANES:(j + 1) * LANES].astype(BF16) for j in range(npair)]
        vbs = [v_ref[0, pl.ds(ks, width), j * LANES:(j + 1) * LANES].astype(BF16) for j in range(npair)]
        add = mask_ref[0, :, pl.ds(ks, width)].astype(F32) if has_mask else None
        _flash_tile(qs, kbs, vbs, biases, shifts, add, keep, m_ref, l_ref, acc_ref)

    n_far = jnp.maximum(i - 1, 0)
    wide = FLASH_FAR // t
    n_wide = n_far // wide

    def wide_body(kj, carry):
        tile(pl.multiple_of(kj * FLASH_FAR, FLASH_FAR), FLASH_FAR, none, far, None)
        return carry

    lax.fori_loop(0, n_wide, wide_body, 0)
    width = FLASH_FAR // 2
    done = n_wide * wide
    while width >= t:
        has = ((n_far - done) * t) >= width
        start = pl.multiple_of(done * t, t)

        @pl.when(has)
        def _(start=start, width=width):
            tile(start, width, none, far, None)

        done = done + jnp.where(has, width // t, 0)
        width //= 2

    row = lax.broadcasted_iota(jnp.int32, (t, 2 * t), 0)
    col = lax.broadcasted_iota(jnp.int32, (t, 2 * t), 1)
    keep = _chunk_of(col) <= _chunk_of(row) + t // CHUNK

    @pl.when(i == 0)
    def _():
        tile(0, t, [bias_ref[k, :, t:2 * t] for k in bias_of], none, keep[:, t:2 * t])

    @pl.when(i >= 1)
    def _():
        tile(pl.multiple_of((i - 1) * t, t), 2 * t, [bias_ref[k] for k in bias_of], none, keep)

    _flash_finish(diff, lam_ref, alam_ref, g_ref, o_ref, l_ref, acc_ref, lane)


def flash_causal(proj, near_bias, rel_bias, lam_init, *, diff, qcol, kcol, vcol, bias_col0,
                 mask=None, a_lambda=None, a_norm_g=None):
    b, t_all, _ = proj.shape
    t = FLASH_T
    npair = 4
    pp = FLASH_PAIRS
    w = pp * LANES
    nb = (1 if diff else 2) * pp
    qb, kb, vb = qcol // w, kcol // w, vcol // w
    bb0 = bias_col0 // nb
    in_specs = [pl.BlockSpec(memory_space=pltpu.SMEM),
                pl.BlockSpec(memory_space=pltpu.SMEM),
                pl.BlockSpec((1, t, w), lambda bi, p, i: (bi, i, qb + p)),
                pl.BlockSpec((1, t_all, w), lambda bi, p, i: (bi, 0, kb + p)),
                pl.BlockSpec((1, t_all, w), lambda bi, p, i: (bi, 0, vb + p)),
                pl.BlockSpec((nb, t, 2 * t), lambda bi, p, i: (bb0 + p, 0, 0))]
    args = [rel_bias, lam_init, proj, proj, proj, near_bias]
    if mask is not None:
        in_specs.append(pl.BlockSpec((1, t, t_all), lambda bi, p, i: (bi, i, 0)))
        args.append(mask)
    if diff:
        in_specs += [pl.BlockSpec((4, A_DIM), lambda bi, p, i: (0, 0)),
                     pl.BlockSpec((1, LANES), lambda bi, p, i: (0, 0))]
        args += [a_lambda, a_norm_g]
    return pl.pallas_call(
        functools.partial(_flash_causal_kernel, diff=diff, has_mask=mask is not None, col0=bias_col0),
        grid=(b, npair // pp, t_all // t),
        in_specs=in_specs,
        out_specs=pl.BlockSpec((1, t, w), lambda bi, p, i: (bi, i, p)),
        out_shape=jax.ShapeDtypeStruct((b, t_all, npair * LANES), F32),
        scratch_shapes=[pltpu.VMEM((2 * pp, t, 1), F32), pltpu.VMEM((2 * pp, t, 1), F32),
                        pltpu.VMEM((2 * pp, t, LANES), F32)],
        compiler_params=_cparams(("arbitrary", "arbitrary", "arbitrary")),
        name="flash_diff" if diff else "flash_sel",
    )(*args)


def _flash_full_kernel(*refs, diff, has_mask, tk):
    lam_ref, q_ref, kc_ref, vc_ref, kn_ref, vn_ref, bias_ref = refs[:7]
    pos = 7
    mask_ref = None
    if has_mask:
        mask_ref = refs[pos]
        pos += 1
    alam_ref = g_ref = None
    if diff:
        alam_ref, g_ref = refs[pos], refs[pos + 1]
        pos += 2
    o_ref, m_ref, l_ref, acc_ref = refs[pos:pos + 4]
    past = kc_ref.shape[2]
    tn = kn_ref.shape[1]
    nb = bias_ref.shape[0]
    lane, qs = _flash_init(q_ref, m_ref, l_ref, acc_ref)

    def tile(kb, vb, c0, width):
        add = mask_ref[0, :, c0:c0 + width].astype(F32) if has_mask else None
        _flash_tile(qs, [kb], [vb], [bias_ref[min(m, nb - 1), :, c0:c0 + width] for m in range(2)], (None, None),
                    add, None, m_ref, l_ref, acc_ref)

    for j in range(past // tk):
        tile(kc_ref[0, 0, j * tk:(j + 1) * tk, :].astype(BF16), vc_ref[0, 0, j * tk:(j + 1) * tk, :].astype(BF16),
             j * tk, tk)
    tile(kn_ref[0].astype(BF16), vn_ref[0].astype(BF16), past, tn)
    _flash_finish(diff, lam_ref, alam_ref, g_ref, o_ref, l_ref, acc_ref, lane)


def flash_full(proj, k_cache, v_cache, layer, full_bias, lam_init, *, diff, qcol, kcol, vcol, bias_col0,
               mask=None, a_lambda=None, a_norm_g=None):
    b, tq, _ = proj.shape
    past = k_cache.shape[2]
    npair = 4
    nb = 1 if diff else 2
    ltot = past + tq
    qb, kb, vb = qcol // LANES, kcol // LANES, vcol // LANES
    bb0 = bias_col0 // nb
    in_specs = [pl.BlockSpec(memory_space=pltpu.SMEM),
                pl.BlockSpec((1, tq, LANES), lambda bi, p: (bi, 0, qb + p)),
                pl.BlockSpec((1, 1, past, LANES), lambda bi, p: (layer, bi, 0, p)),
                pl.BlockSpec((1, 1, past, LANES), lambda bi, p: (layer, bi, 0, p)),
                pl.BlockSpec((1, tq, LANES), lambda bi, p: (bi, 0, kb + p)),
                pl.BlockSpec((1, tq, LANES), lambda bi, p: (bi, 0, vb + p)),
                pl.BlockSpec((nb, tq, ltot), lambda bi, p: (bb0 + p, 0, 0))]
    args = [lam_init, proj, k_cache, v_cache, proj, proj, full_bias]
    if mask is not None:
        in_specs.append(pl.BlockSpec((1, tq, mask.shape[2]), lambda bi, p: (bi, 0, 0)))
        args.append(mask)
    if diff:
        in_specs += [pl.BlockSpec((4, A_DIM), lambda bi, p: (0, 0)),
                     pl.BlockSpec((1, LANES), lambda bi, p: (0, 0))]
        args += [a_lambda, a_norm_g]
    return pl.pallas_call(
        functools.partial(_flash_full_kernel, diff=diff, has_mask=mask is not None, tk=past),
        grid=(b, npair),
        in_specs=in_specs,
        out_specs=pl.BlockSpec((1, tq, LANES), lambda bi, p: (bi, 0, p)),
        out_shape=jax.ShapeDtypeStruct((b, tq, npair * LANES), F32),
        scratch_shapes=[pltpu.VMEM((2, tq, 1), F32), pltpu.VMEM((2, tq, 1), F32), pltpu.VMEM((2, tq, LANES), F32)],
        compiler_params=_cparams(("arbitrary", "arbitrary")),
        name="flash_diff_full" if diff else "flash_sel_full",
    )(*args)


def _index_operands(qi, misc, width):
    tq = qi.shape[0]
    w = misc[:, MISC_IW:MISC_IW + IDX_HEADS] * (IDX_HEADS ** -0.5 * IDX_DIM ** -0.5)
    qh = [qi[:, h * IDX_DIM:(h + 1) * IDX_DIM].astype(BF16) for h in range(IDX_HEADS)]
    wh = [jnp.broadcast_to(w[:, h:h + 1], (tq, width)) for h in range(IDX_HEADS)]
    return qh, wh


def _index_scores(qh, wh, kb):
    isc = None
    for q, w in zip(qh, wh):
        term = w[:, :kb.shape[0]] * jnp.maximum(_dot_nt(q, kb), 0.0)
        isc = term if isc is None else isc + term
    return isc + 0.0


def _sortable(x):
    bits = lax.bitcast_convert_type(x, jnp.int32)
    return jnp.where(bits < 0, bits ^ jnp.int32(0x7FFFFFFF), bits)


def _store_keys(keys_ref, hi_ref, cs, key):
    width = key.shape[1]
    keys_ref[:, pl.ds(cs, width)] = key
    hi_ref[:, pl.ds(cs, width)] = jnp.right_shift(key, 16).astype(jnp.int16)


def _select_topk(keys_ref, hi_ref, lo_ref, nkb, k_sel, out_ref):
    tq = keys_ref.shape[0]

    def block(ref, j):
        return ref[:, pl.ds(pl.multiple_of(j * SEL_KB, SEL_KB), SEL_KB)]

    def count_ge(ref, cand):
        cb = jnp.broadcast_to(cand, (tq, SEL_KB)).astype(jnp.int16)

        def body(j, acc):
            return acc + jnp.where(block(ref, j) >= cb, jnp.int16(1), jnp.int16(0))

        acc = lax.fori_loop(0, nkb, body, jnp.zeros((tq, SEL_KB), jnp.int16))
        return jnp.sum(acc.astype(jnp.int32), axis=-1, keepdims=True)

    def count_gt(ref, t):
        top = -I16_MIN - 1
        return jnp.where(t >= top, 0, count_ge(ref, jnp.minimum(t + 1, top)))

    def search(ref, base):
        zero = jnp.zeros((tq, 1), jnp.int32)
        t0 = jnp.where(base + count_ge(ref, zero) >= k_sel, zero, jnp.full((tq, 1), I16_MIN, jnp.int32))

        def bit_body(it, t):
            cand = t + jnp.left_shift(jnp.int32(1), 14 - it)
            return jnp.where(base + count_ge(ref, cand) >= k_sel, cand, t)

        return lax.fori_loop(0, 15, bit_body, t0)

    hi = search(hi_ref, jnp.zeros((tq, 1), jnp.int32))
    above = count_gt(hi_ref, hi)
    hib = jnp.broadcast_to(hi, (tq, SEL_KB))

    def lo_body(j, carry):
        key = block(keys_ref, j)
        lo = jnp.bitwise_and(key, 0xFFFF) + I16_MIN
        lo = jnp.where(jnp.right_shift(key, 16) == hib, lo, I16_MIN)
        lo_ref[:, pl.ds(pl.multiple_of(j * SEL_KB, SEL_KB), SEL_KB)] = lo.astype(jnp.int16)
        return carry

    lax.fori_loop(0, nkb, lo_body, 0)
    lo = search(lo_ref, above)
    thr = hi * 65536 + (lo - I16_MIN)
    need = k_sel - (above + count_gt(lo_ref, lo))
    need = jnp.where(thr == INT_MIN, 0, need).astype(F32)
    thrb = jnp.broadcast_to(thr, (tq, SEL_KB))
    needb = jnp.broadcast_to(need, (tq, SEL_KB))
    r = lax.broadcasted_iota(jnp.int32, (SEL_KB, 2 * SEL_KB), 0)
    c = lax.broadcasted_iota(jnp.int32, (SEL_KB, 2 * SEL_KB), 1)
    tri = jnp.where((r <= c) | (c >= SEL_KB), 1.0, 0.0).astype(BF16)

    def mask_body(j, offs):
        blk = block(keys_ref, j)
        eq = blk == thrb
        cnt = _dot(jnp.where(eq, 1.0, 0.0).astype(BF16), tri)
        sel = (blk > thrb) | (eq & (offs + cnt[:, :SEL_KB] <= needb))
        out_ref[0, :, pl.ds(pl.multiple_of(j * SEL_KB, SEL_KB), SEL_KB)] = jnp.where(sel, 0.0, NEG).astype(out_ref.dtype)
        return offs + cnt[:, SEL_KB:]

    lax.fori_loop(0, nkb, mask_body, jnp.zeros((tq, SEL_KB), F32))


def _tree_sum(parts):
    while len(parts) > 1:
        parts = [parts[k] + parts[k + 1] for k in range(0, len(parts) - 1, 2)] + parts[len(parts) & ~1:]
    return parts[0]


def _select_causal_kernel(iq_ref, mq_ref, mk_ref, o_ref, keys_ref, hi_ref, lo_ref, *, k_sel):
    tq = iq_ref.shape[1]
    i = pl.program_id(1)
    nkb = (i * tq) // SEL_ROWS + 1
    sub = 16
    qt = jnp.transpose(iq_ref[0])
    mt = jnp.transpose(mq_ref[0])
    qh = [qt[h * IDX_DIM:(h + 1) * IDX_DIM, :].astype(BF16) for h in range(IDX_HEADS)]
    wh = [mt[MISC_IW + h:MISC_IW + h + 1, :] * (IDX_HEADS ** -0.5 * IDX_DIM ** -0.5) for h in range(IDX_HEADS)]
    qpos = lax.broadcasted_iota(jnp.int32, (SEL_ROWS, tq), 1) + i * tq
    kpos = lax.broadcasted_iota(jnp.int32, (SEL_ROWS, tq), 0)

    def rows(j):
        return pl.ds(pl.multiple_of(j * SEL_ROWS, SEL_ROWS), SEL_ROWS)

    def score_body(j, carry):
        kb = mk_ref[0, rows(j), :][:, 0:IDX_DIM].astype(BF16)
        isc = _tree_sum([w * jnp.maximum(_dot(kb, q), 0.0) for q, w in zip(qh, wh)]) + 0.0
        vis = _chunk_of(kpos + j * SEL_ROWS) <= _chunk_of(qpos)
        key = jnp.where(vis, _sortable(isc), INT_MIN)
        keys_ref[rows(j), :] = key
        hi_ref[rows(j), :] = jnp.right_shift(key, 16).astype(jnp.int16)
        return carry

    lax.fori_loop(0, nkb, score_body, 0)

    def count_ge(ref, cand):
        cb = jnp.broadcast_to(cand, (sub, tq)).astype(jnp.int16)

        def body(j, acc):
            blk = ref[rows(j), :]
            hits = [jnp.where(blk[k * sub:(k + 1) * sub] >= cb, jnp.int16(1), jnp.int16(0))
                    for k in range(SEL_ROWS // sub)]
            return acc + _tree_sum(hits)

        acc = lax.fori_loop(0, nkb, body, jnp.zeros((sub, tq), jnp.int16))
        return jnp.sum(acc.astype(jnp.int32), axis=0, keepdims=True)

    def count_gt(ref, t):
        top = -I16_MIN - 1
        return jnp.where(t >= top, 0, count_ge(ref, jnp.minimum(t + 1, top)))

    def search(ref, base):
        zero = jnp.zeros((1, tq), jnp.int32)
        t0 = jnp.where(base + count_ge(ref, zero) >= k_sel, zero, jnp.full((1, tq), I16_MIN, jnp.int32))

        def bit_body(it, t):
            cand = t + jnp.left_shift(jnp.int32(1), 14 - it)
            return jnp.where(base + count_ge(ref, cand) >= k_sel, cand, t)

        return lax.fori_loop(0, 15, bit_body, t0)

    hi = search(hi_ref, jnp.zeros((1, tq), jnp.int32))
    above = count_gt(hi_ref, hi)

    def lo_body(j, carry):
        key = keys_ref[rows(j), :]
        lo = jnp.bitwise_and(key, 0xFFFF) + I16_MIN
        lo = jnp.where(jnp.right_shift(key, 16) == hi, lo, I16_MIN)
        lo_ref[rows(j), :] = lo.astype(jnp.int16)
        return carry

    lax.fori_loop(0, nkb, lo_body, 0)
    lo = search(lo_ref, above)
    thr = hi * 65536 + (lo - I16_MIN)
    need = k_sel - (above + count_gt(lo_ref, lo))
    need = jnp.where(thr == INT_MIN, 0, need).astype(F32)
    r = lax.broadcasted_iota(jnp.int32, (2 * SEL_PRE, SEL_PRE), 0)
    c = lax.broadcasted_iota(jnp.int32, (2 * SEL_PRE, SEL_PRE), 1)
    tri = jnp.where((c <= r) | (r >= SEL_PRE), 1.0, 0.0).astype(BF16)
    r = lax.broadcasted_iota(jnp.int32, (tq, tq), 0)
    c = lax.broadcasted_iota(jnp.int32, (tq, tq), 1)
    eye = jnp.where(r == c, 1.0, 0.0).astype(BF16)

    def mask_body(j, offs):
        blk = keys_ref[rows(j), :]
        parts = [blk[k * SEL_PRE:(k + 1) * SEL_PRE] for k in range(SEL_ROWS // SEL_PRE)]
        eqs = [p == thr for p in parts]
        cnts = [_dot(tri, jnp.where(eq, 1.0, 0.0).astype(BF16)) for eq in eqs]
        sels = []
        for p, eq, cnt in zip(parts, eqs, cnts):
            sel = (p > thr) | (eq & (offs + cnt[:SEL_PRE] <= need))
            sels.append(jnp.where(sel, 1.0, 0.0).astype(BF16))
            offs = offs + cnt[SEL_PRE:SEL_PRE + 1]
        picked = _dot_nt(eye, jnp.concatenate(sels, axis=0))
        o_ref[0, :, rows(j)] = ((picked - 1.0) * -NEG).astype(o_ref.dtype)
        return offs

    lax.fori_loop(0, nkb, mask_body, jnp.zeros((1, tq), F32))


def select_causal(proj):
    b, t, _ = proj.shape
    tq = SEL_TQ
    k_sel = min(TOPK_MAX, t // 4)
    return pl.pallas_call(
        functools.partial(_select_causal_kernel, k_sel=k_sel),
        grid=(b, t // tq),
        in_specs=[pl.BlockSpec((1, tq, IDX_HEADS * IDX_DIM), lambda bi, i: (bi, i, COL_IQ // (IDX_HEADS * IDX_DIM))),
                  pl.BlockSpec((1, tq, LANES), lambda bi, i: (bi, i, COL_MISC // LANES)),
                  pl.BlockSpec((1, t, LANES), lambda bi, i: (bi, 0, COL_MISC // LANES))],
        out_specs=pl.BlockSpec((1, tq, t), lambda bi, i: (bi, i, 0)),
        out_shape=jax.ShapeDtypeStruct((b, t, t), BF16),
        scratch_shapes=[pltpu.VMEM((t, tq), jnp.int32), pltpu.VMEM((t, tq), jnp.int16),
                        pltpu.VMEM((t, tq), jnp.int16)],
        compiler_params=_cparams(("arbitrary", "arbitrary")),
        name="select_causal",
    )(proj, proj, proj)


def _select_full_kernel(iq_ref, mq_ref, kc_ref, o_ref, keys_ref, hi_ref, lo_ref, *, k_sel):
    tq = iq_ref.shape[1]
    past = kc_ref.shape[2]
    nkb = keys_ref.shape[1] // SEL_KB
    misc = mq_ref[0]
    qh, wh = _index_operands(iq_ref[0], misc, SEL_KB)
    for j in range(past // SEL_KB):
        kb = kc_ref[0, 0, j * SEL_KB:(j + 1) * SEL_KB, :].astype(BF16)
        _store_keys(keys_ref, hi_ref, j * SEL_KB, _sortable(_index_scores(qh, wh, kb)))
    _store_keys(keys_ref, hi_ref, past, jnp.full((tq, keys_ref.shape[1] - past), INT_MIN, jnp.int32))
    _store_keys(keys_ref, hi_ref, past, _sortable(_index_scores(qh, wh, misc[:, 0:IDX_DIM].astype(BF16))))
    _select_topk(keys_ref, hi_ref, lo_ref, nkb, k_sel, o_ref)


def select_full(proj, kidx_cache, layer):
    b, tq, _ = proj.shape
    past = kidx_cache.shape[2]
    ltot = past + tq
    lpad = -(-ltot // SEL_KB) * SEL_KB
    k_sel = min(TOPK_MAX, ltot // 4)
    return pl.pallas_call(
        functools.partial(_select_full_kernel, k_sel=k_sel),
        grid=(b,),
        in_specs=[pl.BlockSpec((1, tq, IDX_HEADS * IDX_DIM), lambda bi: (bi, 0, COL_IQ // (IDX_HEADS * IDX_DIM))),
                  pl.BlockSpec((1, tq, LANES), lambda bi: (bi, 0, COL_MISC // LANES)),
                  pl.BlockSpec((1, 1, past, IDX_DIM), lambda bi: (layer, bi, 0, 0))],
        out_specs=pl.BlockSpec((1, tq, lpad), lambda bi: (bi, 0, 0)),
        out_shape=jax.ShapeDtypeStruct((b, tq, lpad), BF16),
        scratch_shapes=[pltpu.VMEM((tq, lpad), jnp.int32), pltpu.VMEM((tq, lpad), jnp.int16),
                        pltpu.VMEM((tq, lpad), jnp.int16)],
        compiler_params=_cparams(("arbitrary",)),
        name="select_full",
    )(proj, proj, kidx_cache)


CONVB_ROWS = 64
CONVB_HEAD = 32


def _convb_kernel(a_ref, gt_ref, init_ref, w_ref, dwb_ref, lng_ref, lnb_ref, o_ref, tail_ref, f_ref, sh_ref):
    tt = a_ref.shape[1]
    lead = CONVB_HEAD - (B_WIDTH - 1)

    @pl.when(pl.program_id(1) == 0)
    def _():
        f_ref[0:CONVB_HEAD, :] = init_ref[0]

    f_ref[CONVB_HEAD:CONVB_HEAD + tt, :] = a_ref[0] * jax.nn.sigmoid(gt_ref[0])
    span = sh_ref.shape[1]
    for ph in range(1, SUBLANES):
        sh_ref[ph] = f_ref[ph:ph + span, :]
    for r in range(tt // CONVB_ROWS):
        acc = None
        for k in range(B_WIDTH):
            ph = (lead + k) % SUBLANES
            s0 = r * CONVB_ROWS + lead + k - ph
            rows = f_ref[s0:s0 + CONVB_ROWS, :] if ph == 0 else sh_ref[ph, s0:s0 + CONVB_ROWS, :]
            term = w_ref[k:k + 1, :] * rows
            acc = term if acc is None else acc + term
        y = acc + dwb_ref[...]
        yc = y - jnp.mean(y, axis=-1, keepdims=True)
        yn = yc * lax.rsqrt(jnp.mean(yc * yc, axis=-1, keepdims=True) + EPS) * lng_ref[...] + lnb_ref[...]
        o_ref[0, r * CONVB_ROWS:(r + 1) * CONVB_ROWS, :] = _silu(yn)
    last = f_ref[tt:tt + CONVB_HEAD, :]
    tail_ref[0] = last
    f_ref[0:CONVB_HEAD, :] = last


def conv_module(proj, init, w, dwb, lng, lnb, tt):
    b, t, _ = proj.shape
    cb = COL_BGLU // B_CH
    vec = pl.BlockSpec((1, B_CH), lambda bi, j: (0, 0))
    return pl.pallas_call(
        _convb_kernel,
        grid=(b, t // tt),
        in_specs=[pl.BlockSpec((1, tt, B_CH), lambda bi, j: (bi, j, cb)),
                  pl.BlockSpec((1, tt, B_CH), lambda bi, j: (bi, j, cb + 1)),
                  pl.BlockSpec((1, CONVB_HEAD, B_CH), lambda bi, j: (bi, 0, 0)),
                  pl.BlockSpec((B_WIDTH, B_CH), lambda bi, j: (0, 0)),
                  vec, vec, vec],
        out_specs=[pl.BlockSpec((1, tt, B_CH), lambda bi, j: (bi, j, 0)),
                   pl.BlockSpec((1, CONVB_HEAD, B_CH), lambda bi, j: (bi, 0, 0))],
        out_shape=[jax.ShapeDtypeStruct((b, t, B_CH), F32), jax.ShapeDtypeStruct((b, CONVB_HEAD, B_CH), F32)],
        scratch_shapes=[pltpu.VMEM((CONVB_HEAD + tt, B_CH), F32),
                        pltpu.VMEM((SUBLANES, tt + CONVB_HEAD - SUBLANES, B_CH), F32)],
        compiler_params=_cparams(("arbitrary", "arbitrary")),
        name="conv_module",
    )(proj, proj, init, w, dwb, lng, lnb)


GDN_HEAD = 8
GDN_BB = 4
INV_BASE = 8


def _unit_lower_inverses(mats):
    n = mats[0].shape[0]
    r = lax.broadcasted_iota(jnp.int32, (n, n), 0)
    c = lax.broadcasted_iota(jnp.int32, (n, n), 1)
    eye = jnp.where(r == c, 1.0, 0.0)

    def same_block(size):
        shift = int(math.log2(size))
        return jnp.right_shift(r, shift) == jnp.right_shift(c, shift)

    diag = same_block(INV_BASE)
    xs = [jnp.where(diag, -a, 0.0) for a in mats]
    ps = [eye + x for x in xs]
    splits = [_split_bf16(x) for x in xs]
    for _ in range(int(math.log2(INV_BASE)) - 1):
        splits = [_split_bf16(_dot_split(s, s)) for s in splits]
        ps = [p + _dot_split(_split_bf16(p), s) for p, s in zip(ps, splits)]
    size = INV_BASE
    while size < n:
        off = same_block(2 * size) & jnp.logical_not(same_block(size))
        lows = [_split_bf16(jnp.where(off, a, 0.0)) for a in mats]
        psplit = [_split_bf16(p) for p in ps]
        mids = [_split_bf16(_dot_split(lo, p)) for lo, p in zip(lows, psplit)]
        ps = [p - _dot_split(ph, mid) for p, ph, mid in zip(ps, psplit, mids)]
        size *= 2
    return ps


def _split_bf16(x):
    hi = x.astype(BF16)
    return hi, (x - hi.astype(F32)).astype(BF16)


def _dot_split(a, b):
    (ah, al), (bh, bl) = a, b
    return _dot(ah, bh) + (_dot(ah, bl) + _dot(al, bh))


def _gdn_kernel(x_ref, z_ref, misc_ref, cinit_ref, s0_ref, cw_ref, alog_ref, dtb_ref, ng_ref,
                o_ref, tail_ref, sout_ref, f_ref, s_ref):
    c = pl.program_id(1)

    @pl.when(c == 0)
    def _():
        f_ref[:, 0:GDN_HEAD, :] = cinit_ref[...]
        s_ref[...] = s0_ref[...]

    bb, cc = x_ref.shape[0], x_ref.shape[1]
    lead = GDN_HEAD - (D_CONV - 1)
    r = lax.broadcasted_iota(jnp.int32, (cc, cc), 0)
    col = lax.broadcasted_iota(jnp.int32, (cc, cc), 1)
    incl = col <= r
    strict = col < r
    lower = jnp.where(incl, 1.0, 0.0)

    ys, betas, gcums, gcum_ts = [], [], [], []
    for bi in range(bb):
        u = x_ref[bi]
        f_ref[bi, GDN_HEAD:GDN_HEAD + cc, :] = u
        y = None
        for j in range(D_CONV):
            term = cw_ref[j:j + 1, :] * f_ref[bi, lead + j:lead + j + cc, :]
            y = term if y is None else y + term
        ys.append(_silu(y))
        last = u[cc - GDN_HEAD:cc, :]
        tail_ref[bi] = last
        f_ref[bi, 0:GDN_HEAD, :] = last
        misc = misc_ref[bi]
        betas.append(jax.nn.sigmoid(misc))
        xg = misc + dtb_ref[...]
        softplus = jnp.maximum(xg, 0.0) + jnp.log(1.0 + jnp.exp(-jnp.abs(xg)))
        g_all = -jnp.exp(alog_ref[...]) * softplus
        gcum = _dot(lower, g_all, HIGHEST)
        gcums.append(gcum)
        gcum_ts.append(jnp.transpose(gcum))

    prob = [(bi, h) for bi in range(bb) for h in range(D_HEADS)]
    qn, kn, vh, bc, gc, gl, decay = [], [], [], [], [], [], []
    for bi, h in prob:
        y = ys[bi]
        q = y[:, h * D_KDIM:(h + 1) * D_KDIM]
        k = y[:, D_QK + h * D_KDIM:D_QK + (h + 1) * D_KDIM]
        vh.append(y[:, 2 * D_QK + h * D_VDIM:2 * D_QK + (h + 1) * D_VDIM])
        qn.append(q * lax.rsqrt(jnp.sum(q * q, axis=-1, keepdims=True) + EPS) * (D_KDIM ** -0.5))
        kn.append(k * lax.rsqrt(jnp.sum(k * k, axis=-1, keepdims=True) + EPS))
        bc.append(betas[bi][:, MISC_DB + h:MISC_DB + h + 1])
        g_col = gcums[bi][:, MISC_DA + h:MISC_DA + h + 1]
        g_row = gcum_ts[bi][MISC_DA + h:MISC_DA + h + 1, :]
        gc.append(g_col)
        gl.append(gcums[bi][cc - 1:cc, MISC_DA + h:MISC_DA + h + 1])
        decay.append(jnp.where(incl, jnp.exp(jnp.where(incl, g_col - g_row, 0.0)), 0.0))
    n = len(prob)
    kb = [x.astype(BF16) for x in kn]
    qb = [x.astype(BF16) for x in qn]
    kk = [_dot_nt(kb[g], kb[g]) for g in range(n)]
    tinv = _unit_lower_inverses([jnp.where(strict, bc[g] * decay[g] * kk[g], 0.0) for g in range(n)])
    s = [s_ref[bi, h] for bi, h in prob]
    sb = [x.astype(BF16) for x in s]
    eg = [jnp.exp(x) for x in gc]
    ks = [_dot(kb[g], sb[g]) for g in range(n)]
    uu = [_dot_split(_split_bf16(tinv[g]), _split_bf16(bc[g] * (vh[g] - eg[g] * ks[g]))) for g in range(n)]
    ub = [x.astype(BF16) for x in uu]
    qk = [(_dot_nt(qb[g], kb[g]) * decay[g]).astype(BF16) for g in range(n)]
    qs = [_dot(qb[g], sb[g]) for g in range(n)]
    o = [eg[g] * qs[g] + _dot(qk[g], ub[g]) for g in range(n)]
    kd = [(kn[g] * jnp.exp(gl[g] - gc[g])).astype(BF16) for g in range(n)]
    s_new = [jnp.exp(gl[g]) * s[g] + _dot_tn(kd[g], ub[g]) for g in range(n)]
    for g, (bi, h) in enumerate(prob):
        s_ref[bi, h] = s_new[g]
        zh = z_ref[bi, :, h * D_VDIM:(h + 1) * D_VDIM]
        o_ref[bi, :, h * D_VDIM:(h + 1) * D_VDIM] = _rms(o[g], ng_ref[...]) * _silu(zh)

    @pl.when(c == pl.num_programs(1) - 1)
    def _():
        sout_ref[...] = s_ref[...]


def gated_delta(proj, cinit, s0, cw, alog, dtb, ng):
    b, t, _ = proj.shape
    cc = min(t, CHUNK)
    bb = math.gcd(b, GDN_BB)
    vec = pl.BlockSpec((1, LANES), lambda bi, c: (0, 0))
    return pl.pallas_call(
        _gdn_kernel,
        grid=(b // bb, t // cc),
        in_specs=[pl.BlockSpec((bb, cc, D_CONV_CH), lambda bi, c: (bi, c, COL_DQKV // D_CONV_CH)),
                  pl.BlockSpec((bb, cc, D_V), lambda bi, c: (bi, c, COL_DZ // D_V)),
                  pl.BlockSpec((bb, cc, LANES), lambda bi, c: (bi, c, COL_MISC // LANES)),
                  pl.BlockSpec((bb, GDN_HEAD, D_CONV_CH), lambda bi, c: (bi, 0, 0)),
                  pl.BlockSpec((bb, D_HEADS, D_KDIM, D_VDIM), lambda bi, c: (bi, 0, 0, 0)),
                  pl.BlockSpec((D_CONV, D_CONV_CH), lambda bi, c: (0, 0)),
                  vec, vec, vec],
        out_specs=[pl.BlockSpec((bb, cc, D_V), lambda bi, c: (bi, c, 0)),
                   pl.BlockSpec((bb, GDN_HEAD, D_CONV_CH), lambda bi, c: (bi, 0, 0)),
                   pl.BlockSpec((bb, D_HEADS, D_KDIM, D_VDIM), lambda bi, c: (bi, 0, 0, 0))],
        out_shape=[jax.ShapeDtypeStruct((b, t, D_V), F32),
                   jax.ShapeDtypeStruct((b, GDN_HEAD, D_CONV_CH), F32),
                   jax.ShapeDtypeStruct((b, D_HEADS, D_KDIM, D_VDIM), F32)],
        scratch_shapes=[pltpu.VMEM((bb, GDN_HEAD + cc, D_CONV_CH), F32),
                        pltpu.VMEM((bb, D_HEADS, D_KDIM, D_VDIM), F32)],
        compiler_params=_cparams(("arbitrary", "arbitrary")),
        name="gated_delta",
    )(proj, proj, proj, cinit, s0, cw, alog, dtb, ng)


def _merge_kernel(oa_ref, ob_ref, oc_ref, od_ref, g0_ref, g1_ref, g2_ref, g3_ref, x_ref, gt_ref, ng_ref,
                  wbr_ref, wout_ref, o_ref):
    bt, tt, d = x_ref.shape
    rows = bt * tt
    merged = None
    for m, (br, gate) in enumerate(((oa_ref, g0_ref), (ob_ref, g1_ref), (oc_ref, g2_ref), (od_ref, g3_ref))):
        term = jax.nn.sigmoid(gate[...].reshape(rows, d)) * _dot(br[...].reshape(rows, BRANCH_W).astype(BF16), wbr_ref[m])
        merged = term if merged is None else merged + term
    mix = _dot(merged.astype(BF16), wout_ref[...])
    o_ref[...] = x_ref[...] + gt_ref[...] * _rms(mix, ng_ref[...]).reshape(bt, tt, d)


def merge_out(branches, proj, x, mod, ng, wbr, wout, bt, tt):
    b, t, d = x.shape
    gb = COL_GATE // d
    br_spec = pl.BlockSpec((bt, tt, BRANCH_W), lambda i, j: (i, j, 0))
    gate_specs = [pl.BlockSpec((bt, tt, d), functools.partial(lambda i, j, m: (i, j, gb + m), m=m))
                  for m in range(N_BRANCH)]
    return pl.pallas_call(
        _merge_kernel,
        grid=(b // bt, t // tt),
        in_specs=[br_spec] * 4 + gate_specs + [
            pl.BlockSpec((bt, tt, d), lambda i, j: (i, j, 0)),
            pl.BlockSpec((bt, 1, d), lambda i, j: (i, 0, 2)),
            pl.BlockSpec((1, d), lambda i, j: (0, 0)),
            pl.BlockSpec((N_BRANCH, BRANCH_W, d), lambda i, j: (0, 0, 0)),
            pl.BlockSpec((d, d), lambda i, j: (0, 0))],
        out_specs=pl.BlockSpec((bt, tt, d), lambda i, j: (i, j, 0)),
        out_shape=jax.ShapeDtypeStruct((b, t, d), F32),
        compiler_params=_cparams(("arbitrary", "arbitrary")),
        name="merge_out",
    )(*branches, proj, proj, proj, proj, x, mod, ng, wbr, wout)


def _mlp_kernel(x_ref, sh_ref, sc_ref, gt_ref, g2_ref, g3_ref, w1_ref, w2_ref, o_ref, h_ref, acc_ref):
    bt, tt, d = x_ref.shape
    f = pl.program_id(2)

    @pl.when(f == 0)
    def _():
        h = _rms(x_ref[...], g2_ref[...]) * (1.0 + sc_ref[...]) + sh_ref[...]
        h_ref[...] = h.reshape(bt * tt, d).astype(BF16)
        acc_ref[...] = jnp.zeros(acc_ref.shape, F32)

    a = jnp.maximum(_dot(h_ref[...], w1_ref[...]), 0.0)
    acc_ref[...] += _dot((a * a).astype(BF16), w2_ref[...])

    @pl.when(f == pl.num_programs(2) - 1)
    def _():
        o_ref[...] = x_ref[...] + gt_ref[...] * _rms(acc_ref[...], g3_ref[...]).reshape(bt, tt, d)


def mlp(x, mod, g2, g3, w1, w2, bt, tt):
    b, t, d = x.shape
    ff = w1.shape[1]
    tf = 1024
    return pl.pallas_call(
        _mlp_kernel,
        grid=(b // bt, t // tt, ff // tf),
        in_specs=[pl.BlockSpec((bt, tt, d), lambda i, j, f: (i, j, 0)),
                  pl.BlockSpec((bt, 1, d), lambda i, j, f: (i, 0, 3)),
                  pl.BlockSpec((bt, 1, d), lambda i, j, f: (i, 0, 4)),
                  pl.BlockSpec((bt, 1, d), lambda i, j, f: (i, 0, 5)),
                  pl.BlockSpec((1, d), lambda i, j, f: (0, 0)),
                  pl.BlockSpec((1, d), lambda i, j, f: (0, 0)),
                  pl.BlockSpec((d, tf), lambda i, j, f: (0, f)),
                  pl.BlockSpec((tf, d), lambda i, j, f: (f, 0))],
        out_specs=pl.BlockSpec((bt, tt, d), lambda i, j, f: (i, j, 0)),
        out_shape=jax.ShapeDtypeStruct((b, t, d), F32),
        scratch_shapes=[pltpu.VMEM((bt * tt, d), BF16), pltpu.VMEM((bt * tt, d), F32)],
        compiler_params=_cparams(("arbitrary", "arbitrary", "arbitrary")),
        name="mlp",
    )(x, mod, mod, mod, g2, g3, w1, w2)


def _combined_in_weight(w_in_l, w_gate_l):
    offs = np.concatenate([[0], np.cumsum(IN_SIZES)])
    (aq, ak, av, bglu, cq, ck, cv, iq, ik, iw, dqkv, dz, db, da) = [
        w_in_l[:, int(offs[i]):int(offs[i + 1])] for i in range(len(IN_SIZES))]
    d = w_in_l.shape[0]
    misc = jnp.concatenate([ik, iw, db, da, jnp.zeros((d, LANES - IDX_DIM - IDX_HEADS - 2 * D_HEADS), F32)], axis=1)
    pad = jnp.zeros((d, COL_GATE - COL_MISC - LANES), F32)
    gates = [w_gate_l[m] for m in range(N_BRANCH)]
    w = jnp.concatenate([aq, ak, av, bglu, cq, ck, cv, dqkv, dz, iq, misc, pad] + gates, axis=1)
    assert w.shape[1] == N_PROJ
    return w.astype(BF16)


def _lane_pad(v, offset):
    return jnp.zeros((1, LANES), F32).at[0, offset:offset + v.shape[0]].set(v)


def _run_group(x, mod, lw, past, bias, rel_bias, tiles):
    b, t, d = x.shape
    bt, tt, conv_tt = tiles
    proj = inproj(x, mod, lw['g'][0:1], lw['w_all'], bt, tt)

    if past is None:
        b_init = jnp.zeros((b, CONVB_HEAD, B_CH), F32)
        d_init = jnp.zeros((b, GDN_HEAD, D_CONV_CH), F32)
        s0 = jnp.zeros((b, D_HEADS, D_KDIM, D_VDIM), F32)
        o_a = flash_causal(proj, bias, rel_bias, lw['lam_init'], diff=True, qcol=COL_AQ, kcol=COL_AK, vcol=COL_AV,
                           bias_col0=0, a_lambda=lw['a_lambda'], a_norm_g=lw['a_norm_g'])
        sel = select_causal(proj)
        o_c = flash_causal(proj, bias, rel_bias, lw['lam_init'], diff=False, qcol=COL_CQ, kcol=COL_CK, vcol=COL_CV,
                           bias_col0=A_HEADS, mask=sel)
    else:
        layer = past['layer']
        lead_b = CONVB_HEAD - (B_WIDTH - 1)
        b_init = jnp.pad(past['b_conv'][layer], ((0, 0), (lead_b, 0), (0, 0)))
        d_init = jnp.pad(past['d_conv'][layer], ((0, 0), (GDN_HEAD - (D_CONV - 1), 0), (0, 0)))
        s0 = past['d_state'][layer]
        o_a = flash_full(proj, past['a_k'], past['a_v'], layer, bias, lw['lam_init'], diff=True,
                         qcol=COL_AQ, kcol=COL_AK, vcol=COL_AV, bias_col0=0,
                         a_lambda=lw['a_lambda'], a_norm_g=lw['a_norm_g'])
        sel = select_full(proj, past['c_kidx'], layer)
        o_c = flash_full(proj, past['c_k'], past['c_v'], layer, bias, lw['lam_init'], diff=False,
                         qcol=COL_CQ, kcol=COL_CK, vcol=COL_CV, bias_col0=A_HEADS, mask=sel)

    o_b, b_tail = conv_module(proj, b_init, lw['b_dw_w'], lw['b_dw_b'], lw['b_ln_g'], lw['b_ln_b'], conv_tt)
    o_d, d_tail, s_new = gated_delta(proj, d_init, s0, lw['d_conv_w'], lw['alog'], lw['dtb'], lw['d_norm_g'])

    mbt, mtt = (1, 256) if t >= 256 else (min(b, 256 // t), t)
    x1 = merge_out((o_a, o_b, o_c, o_d), proj, x, mod, lw['g'][1:2], lw['w_br'], lw['w_out'], mbt, mtt)
    fbt, ftt = (1, 1024) if t >= 1024 else (min(b, 1024 // t), t)
    x2 = mlp(x1, mod, lw['g'][2:3], lw['g'][3:4], lw['w1'], lw['w2'], fbt, ftt)

    new = {
        'a_k': proj[:, :, COL_AK:COL_AK + A_QK].reshape(b, t, A_HEADS, 2 * A_DIM),
        'a_v': proj[:, :, COL_AV:COL_AV + A_QK].reshape(b, t, A_HEADS, 2 * A_DIM),
        'c_k': proj[:, :, COL_CK:COL_CK + C_W].reshape(b, t, C_HEADS, C_DIM),
        'c_v': proj[:, :, COL_CV:COL_CV + C_W].reshape(b, t, C_HEADS, C_DIM),
        'c_kidx': proj[:, :, COL_MISC:COL_MISC + IDX_DIM],
        'b_conv': b_tail[:, CONVB_HEAD - (B_WIDTH - 1):, :],
        'd_conv': d_tail[:, GDN_HEAD - (D_CONV - 1):, :],
        'd_state': s_new,
    }
    return x2, new


def kernel(x_prompt, x_sample, c_prompt, c_sample, cache_a_k, cache_a_v, cache_c_k, cache_c_v, cache_c_kidx, state_b_conv, state_d_conv, state_d_state, rel_bias, ada_w, ada_b, norm_g, w_in, a_lambda, a_norm_g, b_dw_w, b_dw_b, b_ln_g, b_ln_b, d_conv_w, d_a_log, d_dt_bias, d_norm_g, w_gate, w_br, w_out, mlp_w1, mlp_w2):
    depth = w_in.shape[0]
    bp, tp, d = x_prompt.shape
    bs, ts, _ = x_sample.shape
    past_len = cache_a_k.shape[2]

    mod_all = adaln_mod(jnp.concatenate([c_prompt, c_sample], axis=0), ada_w, ada_b)
    bias_p = bias_tiles(rel_bias, FLASH_T, 2 * FLASH_T, FLASH_T, 0)
    bias_s = bias_tiles(rel_bias, ts, past_len + ts, past_len, 0)

    past = {
        'a_k': cache_a_k.reshape(depth, bs, past_len, A_QK),
        'a_v': cache_a_v.reshape(depth, bs, past_len, A_QK),
        'c_k': cache_c_k.reshape(depth, bs, past_len, C_W),
        'c_v': cache_c_v.reshape(depth, bs, past_len, C_W),
        'c_kidx': cache_c_kidx, 'b_conv': state_b_conv, 'd_conv': state_d_conv, 'd_state': state_d_state,
    }
    names = ('a_k', 'a_v', 'c_k', 'c_v', 'c_kidx', 'b_conv', 'd_conv', 'd_state')
    new_p = {n: [] for n in names}
    new_s = {n: [] for n in names}
    xp, xs = x_prompt, x_sample
    for l in range(depth):
        lw = {
            'g': norm_g[l],
            'w_all': _combined_in_weight(w_in[l], w_gate[l]),
            'lam_init': jnp.full((1,), 0.8 - 0.6 * math.exp(-0.3 * l), F32),
            'a_lambda': a_lambda[l],
            'a_norm_g': a_norm_g[l].reshape(1, LANES),
            'b_dw_w': b_dw_w[l], 'b_dw_b': b_dw_b[l].reshape(1, B_CH),
            'b_ln_g': b_ln_g[l].reshape(1, B_CH), 'b_ln_b': b_ln_b[l].reshape(1, B_CH),
            'd_conv_w': d_conv_w[l],
            'alog': _lane_pad(d_a_log[l], MISC_DA), 'dtb': _lane_pad(d_dt_bias[l], MISC_DA),
            'd_norm_g': d_norm_g[l].reshape(1, LANES),
            'w_br': w_br[l].astype(BF16), 'w_out': w_out[l].astype(BF16),
            'w1': mlp_w1[l].astype(BF16), 'w2': mlp_w2[l].astype(BF16),
        }
        mod_p = mod_all[l, :bp].reshape(bp, 1, 6 * d)
        mod_s = mod_all[l, bp:].reshape(bs, 1, 6 * d)
        xp, sp = _run_group(xp, mod_p, lw, None, bias_p, rel_bias, (1, 1024, 256))
        xs, ss = _run_group(xs, mod_s, lw, dict(past, layer=l), bias_s, rel_bias, (min(bs, 1024 // ts), ts, ts))
        for n in names:
            new_p[n].append(sp[n])
            new_s[n].append(ss[n])
    return (xp, xs) + tuple(jnp.stack(new_p[n]) for n in names) + tuple(jnp.stack(new_s[n]) for n in names)
```
